```python
import math
import jax, jax.numpy as jnp
from jax import lax
import numpy as np

D_MODEL = 4096
BATCH = 1
SEQ = 16384
DEPTH = 4

N_BRANCHES = 4
BRANCH_WIDTH = D_MODEL // N_BRANCHES
ATT_HEADS = 8
ATT_HEAD_DIM = BRANCH_WIDTH // ATT_HEADS
Q_BLOCK = 128
SSM_GROUP = 16
SSM_GROUPS = BRANCH_WIDTH // SSM_GROUP
SSM_STATE = 64
DT_MIN = 0.001
DT_MAX = 0.1
CONV_WIDTH = 31
POOL_WINDOWS = (2, 4, 8, 16)
POOL_GROUP = BRANCH_WIDTH // len(POOL_WINDOWS)
GATE_BLOCKS = 16
GATE_BLOCK = D_MODEL // GATE_BLOCKS
D_FF = 3 * D_MODEL // 2
EPS = 1e-6

OFF_Q = 0
OFF_K = OFF_Q + BRANCH_WIDTH
OFF_V = OFF_K + BRANCH_WIDTH
OFF_F = OFF_V + BRANCH_WIDTH
OFF_SSM = OFF_F + ATT_HEADS
OFF_CONV_VAL = OFF_SSM + BRANCH_WIDTH
OFF_CONV_GATE = OFF_CONV_VAL + BRANCH_WIDTH
OFF_POOL = OFF_CONV_GATE + BRANCH_WIDTH
IN_WIDTH = OFF_POOL + BRANCH_WIDTH

kernel_name = 'hybrid_gated_fox_s5_conv_pool_macaron'


def rms_norm(x, g):
    xf = x.astype(jnp.float32)
    y = xf * lax.rsqrt(jnp.mean(xf * xf, axis=-1, keepdims=True) + EPS)
    return (y * g.astype(jnp.float32)).astype(x.dtype)


def layer_norm(x, g, b):
    xf = x.astype(jnp.float32)
    mu = jnp.mean(xf, axis=-1, keepdims=True)
    xc = xf - mu
    var = jnp.mean(xc * xc, axis=-1, keepdims=True)
    y = xc * lax.rsqrt(var + EPS) * g.astype(jnp.float32) + b.astype(jnp.float32)
    return y.astype(x.dtype)


def swiglu(h, w_gate, w_up, w_down):
    return (jax.nn.silu(h @ w_gate) * (h @ w_up)) @ w_down


def forgetting_attention(q, k, v, f_logit, b_f):
    B, S, _ = q.shape
    q = q.reshape(B, S, ATT_HEADS, ATT_HEAD_DIM).transpose(0, 2, 1, 3)
    k = k.reshape(B, S, ATT_HEADS, ATT_HEAD_DIM).transpose(0, 2, 1, 3)
    v = v.reshape(B, S, ATT_HEADS, ATT_HEAD_DIM).transpose(0, 2, 1, 3)
    log_f = jax.nn.log_sigmoid((f_logit + b_f).astype(jnp.float32))
    c = lax.cumsum(log_f, axis=1).transpose(0, 2, 1)
    nb = S // Q_BLOCK
    qb = q.reshape(B, ATT_HEADS, nb, Q_BLOCK, ATT_HEAD_DIM).transpose(2, 0, 1, 3, 4)
    cb = c.reshape(B, ATT_HEADS, nb, Q_BLOCK).transpose(2, 0, 1, 3)
    kpos = jnp.arange(S)
    scale = ATT_HEAD_DIM ** -0.5

    def block(args):
        q_i, c_i, i = args
        logits = jnp.einsum('bhqd,bhkd->bhqk', q_i, k,
                            preferred_element_type=jnp.float32) * scale
        logits = logits + c_i[..., None] - c[:, :, None, :]
        qpos = i * Q_BLOCK + jnp.arange(Q_BLOCK)
        logits = jnp.where(kpos[None, :] <= qpos[:, None], logits, -jnp.inf)
        p = jax.nn.softmax(logits, axis=-1)
        return jnp.einsum('bhqk,bhkd->bhqd', p.astype(v.dtype), v)

    o = lax.map(block, (qb, cb, jnp.arange(nb)))
    return o.transpose(1, 0, 3, 2, 4).reshape(B, S, ATT_HEADS * ATT_HEAD_DIM)


def s5_layer(u, lam_re, lam_im, log_dt, b_re, b_im, c_re, c_im, d_skip, w_glu, b_glu):
    B, S, _ = u.shape
    uf = u.astype(jnp.float32).reshape(B, S, SSM_GROUPS, SSM_GROUP)
    lam = lax.complex(lam_re.astype(jnp.float32), lam_im.astype(jnp.float32))
    dt = jnp.exp(log_dt.astype(jnp.float32))[:, None]
    lam_bar = jnp.exp(lam * dt)
    b = lax.complex(b_re.astype(jnp.float32), b_im.astype(jnp.float32))
    b_bar = ((lam_bar - 1.0) / lam)[..., None] * b
    bu = jnp.einsum('gph,bsgh->bsgp', b_bar, uf.astype(jnp.complex64))
    a = jnp.broadcast_to(lam_bar, bu.shape)

    def combine(left, right):
        a_l, x_l = left
        a_r, x_r = right
        return a_r * a_l, a_r * x_l + x_r

    _, xs = lax.associative_scan(combine, (a, bu), axis=1)
    cm = lax.complex(c_re.astype(jnp.float32), c_im.astype(jnp.float32))
    y = jnp.real(jnp.einsum('ghp,bsgp->bsgh', cm, xs))
    y = y + d_skip.astype(jnp.float32).reshape(SSM_GROUPS, SSM_GROUP) * uf
    y = jax.nn.gelu(y.reshape(B, S, SSM_GROUPS * SSM_GROUP))
    y = y * jax.nn.sigmoid(y @ w_glu.astype(jnp.float32) + b_glu.astype(jnp.float32))
    return y.astype(u.dtype)


def conformer_conv(val, gate, w_dw, b_dw, ln_g, ln_b):
    a = val * jax.nn.sigmoid(gate)
    y = lax.conv_general_dilated(
        a, w_dw[:, None, :].astype(a.dtype), window_strides=(1,),
        padding=((CONV_WIDTH - 1, 0),),
        dimension_numbers=('NWC', 'WIO', 'NWC'),
        feature_group_count=a.shape[-1]) + b_dw
    return jax.nn.silu(layer_norm(y, ln_g, ln_b))


def multiscale_pool(p, w_pool, b_pool, scale):
    B, S, _ = p.shape
    pf = p.astype(jnp.float32).reshape(B, S, len(POOL_WINDOWS), POOL_GROUP)
    cs = lax.cumsum(pf, axis=1)
    pos = jnp.arange(1, S + 1, dtype=jnp.float32)
    outs = []
    for g, w in enumerate(POOL_WINDOWS):
        c_g = cs[:, :, g]
        lag = jnp.pad(c_g, ((0, 0), (w, 0), (0, 0)))[:, :S]
        mean = (c_g - lag) / jnp.minimum(pos, w)[None, :, None]
        outs.append(mean - pf[:, :, g])
    z = jnp.stack(outs, axis=2)
    z = jnp.einsum('bsgc,gcd->bsgd', z, w_pool.astype(jnp.float32)) + b_pool.astype(jnp.float32)
    z = z.reshape(B, S, len(POOL_WINDOWS) * POOL_GROUP) * scale.astype(jnp.float32)
    return z.astype(p.dtype)


def hybrid_mixer(u, w_in, forget_bias, ssm_lam_re, ssm_lam_im, ssm_log_dt, ssm_b_re,
                 ssm_b_im, ssm_c_re, ssm_c_im, ssm_d, ssm_w_glu, ssm_b_glu, conv_w,
                 conv_b, conv_ln_g, conv_ln_b, pool_w, pool_b, pool_scale,
                 branch_w_up, gate_w, gate_b, w_out):
    B, S, _ = u.shape
    z = u @ w_in
    y_a = forgetting_attention(z[..., OFF_Q:OFF_K], z[..., OFF_K:OFF_V],
                               z[..., OFF_V:OFF_F], z[..., OFF_F:OFF_SSM], forget_bias)
    y_b = s5_layer(z[..., OFF_SSM:OFF_CONV_VAL], ssm_lam_re, ssm_lam_im, ssm_log_dt,
                   ssm_b_re, ssm_b_im, ssm_c_re, ssm_c_im, ssm_d, ssm_w_glu, ssm_b_glu)
    y_c = conformer_conv(z[..., OFF_CONV_VAL:OFF_CONV_GATE], z[..., OFF_CONV_GATE:OFF_POOL],
                         conv_w, conv_b, conv_ln_g, conv_ln_b)
    y_d = multiscale_pool(z[..., OFF_POOL:IN_WIDTH], pool_w, pool_b, pool_scale)
    u_blk = u.reshape(B, S, GATE_BLOCKS, GATE_BLOCK)
    merged = None
    for n, y in enumerate((y_a, y_b, y_c, y_d)):
        g = jax.nn.sigmoid(
            jnp.einsum('bskc,kcd->bskd', u_blk, gate_w[n]).reshape(B, S, D_MODEL) + gate_b[n])
        term = g * (y @ branch_w_up[n])
        merged = term if merged is None else merged + term
    return merged @ w_out


def setup_inputs(seed: int = 0) -> dict:
    key = jax.random.key(seed)
    ks = iter(jax.random.split(key, 48))
    f32 = jnp.float32

    def nrm(shape, scale):
        return jax.random.normal(next(ks), shape, f32) * scale

    def gain(shape):
        return 1.0 + nrm(shape, 0.02)

    L, D, F, W = DEPTH, D_MODEL, D_FF, BRANCH_WIDTH
    G, P, Hg = SSM_GROUPS, SSM_STATE, SSM_GROUP
    inp = {}
    inp['x'] = nrm((BATCH, SEQ, D), 1.0)
    inp['ffn1_pre_g'] = gain((L, D))
    inp['ffn1_w_gate'] = nrm((L, D, F), D ** -0.5)
    inp['ffn1_w_up'] = nrm((L, D, F), D ** -0.5)
    inp['ffn1_w_down'] = nrm((L, F, D), F ** -0.5)
    inp['ffn1_post_g'] = gain((L, D))
    inp['mix_pre_g'] = gain((L, D))
    inp['w_in'] = nrm((L, D, IN_WIDTH), D ** -0.5)
    inp['forget_bias'] = nrm((L, ATT_HEADS), 0.1)
    inp['ssm_lam_re'] = -0.5 + nrm((L, G, P), 0.01)
    inp['ssm_lam_im'] = math.pi * jnp.arange(P, dtype=f32) + nrm((L, G, P), 0.01)
    inp['ssm_log_dt'] = jax.random.uniform(next(ks), (L, G), f32,
                                           minval=math.log(DT_MIN), maxval=math.log(DT_MAX))
    inp['ssm_b_re'] = nrm((L, G, P, Hg), (2 * Hg) ** -0.5)
    inp['ssm_b_im'] = nrm((L, G, P, Hg), (2 * Hg) ** -0.5)
    inp['ssm_c_re'] = nrm((L, G, Hg, P), P ** -0.5)
    inp['ssm_c_im'] = nrm((L, G, Hg, P), P ** -0.5)
    inp['ssm_d'] = nrm((L, W), 1.0)
    inp['ssm_w_glu'] = nrm((L, W, W), W ** -0.5)
    inp['ssm_b_glu'] = nrm((L, W), 0.01)
    inp['conv_w'] = nrm((L, CONV_WIDTH, W), CONV_WIDTH ** -0.5)
    inp['conv_b'] = nrm((L, W), 0.01)
    inp['conv_ln_g'] = gain((L, W))
    inp['conv_ln_b'] = nrm((L, W), 0.01)
    inp['pool_w'] = nrm((L, len(POOL_WINDOWS), POOL_GROUP, POOL_GROUP), POOL_GROUP ** -0.5)
    inp['pool_b'] = nrm((L, len(POOL_WINDOWS), POOL_GROUP), 0.01)
    inp['pool_scale'] = gain((L, W))
    inp['branch_w_up'] = nrm((L, N_BRANCHES, W, D), W ** -0.5)
    inp['gate_w'] = nrm((L, N_BRANCHES, GATE_BLOCKS, GATE_BLOCK, GATE_BLOCK), GATE_BLOCK ** -0.5)
    inp['gate_b'] = nrm((L, N_BRANCHES, D), 0.01)
    inp['w_out'] = nrm((L, D, D), D ** -0.5)
    inp['mix_post_g'] = gain((L, D))
    inp['ffn2_pre_g'] = gain((L, D))
    inp['ffn2_w_gate'] = nrm((L, D, F), D ** -0.5)
    inp['ffn2_w_up'] = nrm((L, D, F), D ** -0.5)
    inp['ffn2_w_down'] = nrm((L, F, D), F ** -0.5)
    inp['ffn2_post_g'] = gain((L, D))
    return inp


def reference(x, ffn1_pre_g, ffn1_w_gate, ffn1_w_up, ffn1_w_down, ffn1_post_g,
              mix_pre_g, w_in, forget_bias, ssm_lam_re, ssm_lam_im, ssm_log_dt,
              ssm_b_re, ssm_b_im, ssm_c_re, ssm_c_im, ssm_d, ssm_w_glu, ssm_b_glu,
              conv_w, conv_b, conv_ln_g, conv_ln_b, pool_w, pool_b, pool_scale,
              branch_w_up, gate_w, gate_b, w_out, mix_post_g,
              ffn2_pre_g, ffn2_w_gate, ffn2_w_up, ffn2_w_down, ffn2_post_g):
    h = x
    for l in range(DEPTH):
        t = swiglu(rms_norm(h, ffn1_pre_g[l]), ffn1_w_gate[l], ffn1_w_up[l], ffn1_w_down[l])
        h = h + 0.5 * rms_norm(t, ffn1_post_g[l])
        t = hybrid_mixer(rms_norm(h, mix_pre_g[l]), w_in[l], forget_bias[l],
                         ssm_lam_re[l], ssm_lam_im[l], ssm_log_dt[l], ssm_b_re[l],
                         ssm_b_im[l], ssm_c_re[l], ssm_c_im[l], ssm_d[l], ssm_w_glu[l],
                         ssm_b_glu[l], conv_w[l], conv_b[l], conv_ln_g[l], conv_ln_b[l],
                         pool_w[l], pool_b[l], pool_scale[l], branch_w_up[l],
                         gate_w[l], gate_b[l], w_out[l])
        h = h + rms_norm(t, mix_post_g[l])
        t = swiglu(rms_norm(h, ffn2_pre_g[l]), ffn2_w_gate[l], ffn2_w_up[l], ffn2_w_down[l])
        h = h + 0.5 * rms_norm(t, ffn2_post_g[l])
    return h
```

```python
import functools
import math

import numpy as np
import jax
import jax.numpy as jnp
from jax import lax
from jax.experimental import pallas as pl
from jax.experimental.pallas import tpu as pltpu

F32 = jnp.float32
BF16 = jnp.bfloat16

D_MODEL = 4096
BRANCH_WIDTH = 1024
ATT_HEADS = 8
ATT_HEAD_DIM = 128
SSM_GROUP = 16
SSM_GROUPS = 64
SSM_STATE = 64
CONV_WIDTH = 31
POOL_WINDOWS = (2, 4, 8, 16)
POOL_GROUP = 256
GATE_BLOCKS = 16
GATE_BLOCK = 256
D_FF = 6144
EPS = 1e-6

V7X_VMEM_BYTES = 64 * 1024 * 1024
VMEM_LIMIT_BYTES = V7X_VMEM_BYTES - 8 * 1024 * 1024
SUBLANES = 8
LANES = 128

Z_Q, Z_K, Z_V, Z_SSM, Z_CVAL, Z_CGATE, Z_POOL = (n * BRANCH_WIDTH for n in range(7))
Z_WIDTH = 7 * BRANCH_WIDTH
HEAD_ROWS = 16

SSM_CHUNK = 256
SSM_SUB = SSM_CHUNK // SUBLANES
SSM_SLABS = 4
SSM_SLAB_IN = BRANCH_WIDTH // SSM_SLABS
SSM_STATES = SSM_GROUPS * SSM_STATE
SSM_SLAB_STATE = SSM_STATES // SSM_SLABS
SSM_STRIP = 512

CONV_HALO = 32
POOL_HALO = 16

NEG_BIG = -1e30


def _params(semantics):
    return pltpu.CompilerParams(dimension_semantics=semantics, vmem_limit_bytes=VMEM_LIMIT_BYTES)


def _sigmoid(x):
    return 1.0 / (1.0 + jnp.exp(-x))


def _rms_scale(x):
    return lax.rsqrt(jnp.mean(x * x, axis=-1, keepdims=True) + EPS)


def _ffn_up_kernel(h_ref, g_ref, wg_ref, wu_ref, o_ref, xn_ref):
    @pl.when(pl.program_id(1) == 0)
    def _():
        x = h_ref[...]
        xn_ref[...] = (x * _rms_scale(x) * g_ref[...]).astype(BF16)

    xn = xn_ref[...]
    a = jnp.dot(xn, wg_ref[...], preferred_element_type=F32)
    b = jnp.dot(xn, wu_ref[...], preferred_element_type=F32)
    o_ref[...] = (a * _sigmoid(a) * b).astype(BF16)


def _ffn_up(h, g, wg, wu, *, tm=512, tn=512):
    s, d = h.shape
    f = wg.shape[1]
    return pl.pallas_call(
        _ffn_up_kernel,
        out_shape=jax.ShapeDtypeStruct((s, f), BF16),
        grid=(s // tm, f // tn),
        in_specs=[
            pl.BlockSpec((tm, d), lambda i, j: (i, 0)),
            pl.BlockSpec((1, d), lambda i, j: (0, 0)),
            pl.BlockSpec((d, tn), lambda i, j: (0, j)),
            pl.BlockSpec((d, tn), lambda i, j: (0, j)),
        ],
        out_specs=pl.BlockSpec((tm, tn), lambda i, j: (i, j)),
        scratch_shapes=[pltpu.VMEM((tm, d), BF16)],
        compiler_params=_params(("parallel", "arbitrary")),
        name="ffn_up",
    )(h, g.reshape(1, d), wg, wu)


def _proj_norm_res_kernel(a_ref, w_ref, g_ref, h_ref, o_ref, *, alpha, col_chunk, row_chunk):
    k = pl.program_id(1)
    tm, d = o_ref.shape
    a = a_ref[...]
    for c0 in range(0, d, col_chunk):
        cols = slice(c0, c0 + col_chunk)
        t = jnp.dot(a, w_ref[:, cols], preferred_element_type=F32)

        @pl.when(k == 0)
        def _():
            o_ref[:, cols] = t

        @pl.when(k > 0)
        def _():
            o_ref[:, cols] += t

    @pl.when(k == pl.num_programs(1) - 1)
    def _():
        for r0 in range(0, tm, row_chunk):
            rows = slice(r0, r0 + row_chunk)
            acc = o_ref[rows, :]
            o_ref[rows, :] = h_ref[rows, :] + alpha * (acc * _rms_scale(acc) * g_ref[...])


def _proj_norm_res(a, w, g, h, alpha, *, tm=512, tk=512):
    s, kdim = a.shape
    d = w.shape[1]
    return pl.pallas_call(
        functools.partial(_proj_norm_res_kernel, alpha=alpha, col_chunk=1024, row_chunk=128),
        out_shape=jax.ShapeDtypeStruct((s, d), F32),
        grid=(s // tm, kdim // tk),
        in_specs=[
            pl.BlockSpec((tm, tk), lambda i, k: (i, k)),
            pl.BlockSpec((tk, d), lambda i, k: (k, 0)),
            pl.BlockSpec((1, d), lambda i, k: (0, 0)),
            pl.BlockSpec((tm, d), lambda i, k: (i, 0)),
        ],
        out_specs=pl.BlockSpec((tm, d), lambda i, k: (i, 0)),
        compiler_params=_params(("parallel", "arbitrary")),
        name="proj_norm_res",
    )(a, w, g.reshape(1, d), h)


def _in_proj_kernel(h_ref, g_ref, w_ref, z_ref, u_ref):
    @pl.when(pl.program_id(1) == 0)
    def _():
        x = h_ref[...]
        u_ref[...] = (x * _rms_scale(x) * g_ref[...]).astype(BF16)

    z_ref[...] = jnp.dot(u_ref[...], w_ref[...], preferred_element_type=F32).astype(BF16)


def _in_proj(h, g, w, *, tm=512, tn=512):
    s, d = h.shape
    n = w.shape[1]
    return pl.pallas_call(
        _in_proj_kernel,
        out_shape=(jax.ShapeDtypeStruct((s, n), BF16), jax.ShapeDtypeStruct((s, d), BF16)),
        grid=(s // tm, n // tn),
        in_specs=[
            pl.BlockSpec((tm, d), lambda i, j: (i, 0)),
            pl.BlockSpec((1, d), lambda i, j: (0, 0)),
            pl.BlockSpec((d, tn), lambda i, j: (0, j)),
        ],
        out_specs=(
            pl.BlockSpec((tm, tn), lambda i, j: (i, j)),
            pl.BlockSpec((tm, d), lambda i, j: (i, 0)),
        ),
        compiler_params=_params(("parallel", "arbitrary")),
        name="in_proj",
    )(h, g.reshape(1, d), w)


def _split_bf16(x, parts=3):
    out = []
    for _ in range(parts):
        p = x.astype(BF16)
        out.append(p)
        x = x - p.astype(F32)
    return out


def _forget_kernel(u_ref, wf_ref, bf_ref, tri_ref, c_ref, carry_ref):
    @pl.when(pl.program_id(0) == 0)
    def _():
        carry_ref[...] = jnp.zeros_like(carry_ref)

    f = lax.dot_general(wf_ref[...], u_ref[...], (((1,), (1,)), ((), ())),
                        preferred_element_type=F32)
    x = f + bf_ref[...]
    log_f = jnp.minimum(x, 0.0) - jnp.log(1.0 + jnp.exp(-jnp.abs(x)))
    tri = tri_ref[...]
    cs = sum(jnp.dot(p, tri, preferred_element_type=F32) for p in _split_bf16(log_f))
    c = cs + carry_ref[...]
    c_ref[...] = c
    carry_ref[...] = c[:, c.shape[1] - 1:]


def _forget_cumsum(u, wf_t, bf, *, tm=512):
    s, d = u.shape
    tri = jnp.asarray(np.triu(np.ones((tm, tm), np.float32)), BF16)
    return pl.pallas_call(
        _forget_kernel,
        out_shape=jax.ShapeDtypeStruct((HEAD_ROWS, s), F32),
        grid=(s // tm,),
        in_specs=[
            pl.BlockSpec((tm, d), lambda i: (i, 0)),
            pl.BlockSpec((HEAD_ROWS, d), lambda i: (0, 0)),
            pl.BlockSpec((HEAD_ROWS, 1), lambda i: (0, 0)),
            pl.BlockSpec((tm, tm), lambda i: (0, 0)),
        ],
        out_specs=pl.BlockSpec((HEAD_ROWS, tm), lambda i: (0, i)),
        scratch_shapes=[pltpu.VMEM((HEAD_ROWS, 1), F32)],
        compiler_params=_params(("arbitrary",)),
        name="forget_cumsum",
    )(u, wf_t, bf, tri)


def _attn_kernel(q_ref, k_ref, v_ref, c_ref, o_ref, m_ref, l_ref, acc_ref, *, tq):
    h = pl.program_id(0)
    i = pl.program_id(1)
    q = q_ref[...]
    start = pl.multiple_of(i * tq, tq)
    c_q = c_ref[pl.ds(h, 1), pl.ds(start, tq)]
    c_base = c_q[:, 0:1]
    nt_dims = (((1,), (1,)), ((), ()))

    k_d = k_ref[pl.ds(start, tq), :]
    s = lax.dot_general(q, k_d, nt_dims, preferred_element_type=F32) - (c_q - c_base)
    rows = lax.broadcasted_iota(jnp.int32, (tq, tq), 0)
    cols = lax.broadcasted_iota(jnp.int32, (tq, tq), 1)
    s = jnp.where(cols <= rows, s, NEG_BIG)
    m = jnp.max(s, axis=1, keepdims=True)
    p = jnp.exp(s - m)
    m_ref[...] = m
    l_ref[...] = jnp.sum(p, axis=1, keepdims=True)
    acc_ref[...] = jnp.dot(p.astype(BF16), v_ref[pl.ds(start, tq), :], preferred_element_type=F32)

    def body(j, carry):
        off = pl.multiple_of(j * tq, tq)
        k_j = k_ref[pl.ds(off, tq), :]
        c_k = c_ref[pl.ds(h, 1), pl.ds(off, tq)] - c_base
        s = lax.dot_general(q, k_j, nt_dims, preferred_element_type=F32) - c_k
        m_old = m_ref[...]
        m_new = jnp.maximum(m_old, jnp.max(s, axis=1, keepdims=True))
        alpha = jnp.exp(m_old - m_new)
        p = jnp.exp(s - m_new)
        l_ref[...] = alpha * l_ref[...] + jnp.sum(p, axis=1, keepdims=True)
        acc_ref[...] = alpha * acc_ref[...] + jnp.dot(p.astype(BF16), v_ref[pl.ds(off, tq), :],
                                                      preferred_element_type=F32)
        m_ref[...] = m_new
        return carry

    lax.fori_loop(0, i, body, 0)
    o_ref[...] = (acc_ref[...] / l_ref[...]).astype(BF16)


def _attention(z, c, *, tq=256):
    s = z.shape[0]
    dh = ATT_HEAD_DIM
    return pl.pallas_call(
        functools.partial(_attn_kernel, tq=tq),
        out_shape=jax.ShapeDtypeStruct((s, BRANCH_WIDTH), BF16),
        grid=(ATT_HEADS, s // tq),
        in_specs=[
            pl.BlockSpec((tq, dh), lambda h, i: (i, Z_Q // dh + h)),
            pl.BlockSpec((s, dh), lambda h, i: (0, Z_K // dh + h)),
            pl.BlockSpec((s, dh), lambda h, i: (0, Z_V // dh + h)),
            pl.BlockSpec((HEAD_ROWS, s), lambda h, i: (0, 0)),
        ],
        out_specs=pl.BlockSpec((tq, dh), lambda h, i: (i, h)),
        scratch_shapes=[pltpu.VMEM((tq, 1), F32), pltpu.VMEM((tq, 1), F32), pltpu.VMEM((tq, dh), F32)],
        compiler_params=_params(("parallel", "arbitrary")),
        name="forget_attention",
    )(z, z, z, c)


def _cmul(ar, ai, br, bi):
    return ar * br - ai * bi, ar * bi + ai * br


def _gelu_tanh(x):
    return 0.5 * x * (1.0 + jnp.tanh(math.sqrt(2.0 / math.pi) * (x + 0.044715 * (x * x * x))))


def _ssm_kernel(u_ref, perm_ref, perm_t_ref, b_re_ref, b_im_ref, c_re_ref, c_im_ref,
                lam_ref, lam_sub_ref, pow_sub_ref, pow_step_ref, d_ref, wglu_ref, bglu_ref,
                o_ref, x_re_ref, x_im_ref, cin_ref, carry_ref):
    @pl.when(pl.program_id(0) == 0)
    def _():
        carry_ref[...] = jnp.zeros_like(carry_ref)

    n_sub = SSM_SUB
    n_strips = SSM_STATES // SSM_STRIP
    u_p = jnp.dot(perm_ref[...], u_ref[...], preferred_element_type=F32).astype(BF16)

    for sl in range(SSM_SLABS):
        u_sl = u_p[:, sl * SSM_SLAB_IN:(sl + 1) * SSM_SLAB_IN]
        cols = slice(sl * SSM_SLAB_STATE, (sl + 1) * SSM_SLAB_STATE)
        x_re_ref[:, cols] = jnp.dot(u_sl, b_re_ref[sl], preferred_element_type=F32)
        x_im_ref[:, cols] = jnp.dot(u_sl, b_im_ref[sl], preferred_element_type=F32)

    def scan_strip(st, carry):
        lanes = pl.ds(pl.multiple_of(st * SSM_STRIP, SSM_STRIP), SSM_STRIP)
        lr = jnp.broadcast_to(lam_ref[0:1, lanes], (SUBLANES, SSM_STRIP))
        li = jnp.broadcast_to(lam_ref[1:2, lanes], (SUBLANES, SSM_STRIP))
        sr = x_re_ref[0:SUBLANES, lanes]
        si = x_im_ref[0:SUBLANES, lanes]
        for i in range(1, n_sub):
            rows = slice(i * SUBLANES, (i + 1) * SUBLANES)
            pr, pi = _cmul(lr, li, sr, si)
            sr = pr + x_re_ref[rows, lanes]
            si = pi + x_im_ref[rows, lanes]
            x_re_ref[rows, lanes] = sr
            x_im_ref[rows, lanes] = si
        return carry

    lax.fori_loop(0, n_strips, scan_strip, 0)

    last = slice((n_sub - 1) * SUBLANES, n_sub * SUBLANES)
    tr = x_re_ref[last, :]
    ti = x_im_ref[last, :]
    row = lax.broadcasted_iota(jnp.int32, (SUBLANES, SSM_STATES), 0)
    for lvl, dist in enumerate((1, 2, 4)):
        ar = lam_sub_ref[2 * lvl:2 * lvl + 1, :]
        ai = lam_sub_ref[2 * lvl + 1:2 * lvl + 2, :]
        shr = jnp.where(row >= dist, pltpu.roll(tr, dist, 0), 0.0)
        shi = jnp.where(row >= dist, pltpu.roll(ti, dist, 0), 0.0)
        pr, pi = _cmul(ar, ai, shr, shi)
        tr = tr + pr
        ti = ti + pi
    in_r = carry_ref[0:1, :]
    in_i = carry_ref[1:2, :]
    pr, pi = _cmul(pow_sub_ref[0:SUBLANES, :], pow_sub_ref[SUBLANES:2 * SUBLANES, :], in_r, in_i)
    tr = tr + pr
    ti = ti + pi
    cin_ref[0:SUBLANES, :] = jnp.where(row >= 1, pltpu.roll(tr, 1, 0), in_r)
    cin_ref[SUBLANES:2 * SUBLANES, :] = jnp.where(row >= 1, pltpu.roll(ti, 1, 0), in_i)
    carry_ref[0:1, :] = tr[SUBLANES - 1:SUBLANES, :]
    carry_ref[1:2, :] = ti[SUBLANES - 1:SUBLANES, :]

    def fix_strip(st, carry):
        lanes = pl.ds(pl.multiple_of(st * SSM_STRIP, SSM_STRIP), SSM_STRIP)
        cr = cin_ref[0:SUBLANES, lanes]
        ci = cin_ref[SUBLANES:2 * SUBLANES, lanes]
        for i in range(n_sub):
            rows = slice(i * SUBLANES, (i + 1) * SUBLANES)
            pr, pi = _cmul(pow_step_ref[i:i + 1, lanes], pow_step_ref[n_sub + i:n_sub + i + 1, lanes], cr, ci)
            x_re_ref[rows, lanes] = x_re_ref[rows, lanes] + pr
            x_im_ref[rows, lanes] = x_im_ref[rows, lanes] + pi
        return carry

    lax.fori_loop(0, n_strips, fix_strip, 0)

    ys = []
    for sl in range(SSM_SLABS):
        cols = slice(sl * SSM_SLAB_STATE, (sl + 1) * SSM_SLAB_STATE)
        ys.append(jnp.dot(x_re_ref[:, cols].astype(BF16), c_re_ref[sl], preferred_element_type=F32)
                  - jnp.dot(x_im_ref[:, cols].astype(BF16), c_im_ref[sl], preferred_element_type=F32))
    y = jnp.concatenate(ys, axis=1) + d_ref[...] * u_p.astype(F32)
    y = _gelu_tanh(y)
    gate = jnp.dot(y.astype(BF16), wglu_ref[...], preferred_element_type=F32) + bglu_ref[...]
    y = (y * _sigmoid(gate)).astype(BF16)
    o_ref[...] = jnp.dot(perm_t_ref[...], y, preferred_element_type=F32).astype(BF16)


def _ssm_tables(lam_re, lam_im, log_dt, b_re, b_im, c_re, c_im):
    dt = jnp.exp(log_dt)[:, None]
    a_re = lam_re * dt
    a_im = lam_im * dt

    def lam_pow(n):
        mag = jnp.exp(a_re * n)
        return (mag * jnp.cos(a_im * n)).reshape(-1), (mag * jnp.sin(a_im * n)).reshape(-1)

    lb_re = jnp.exp(a_re) * jnp.cos(a_im)
    lb_im = jnp.exp(a_re) * jnp.sin(a_im)
    den = lam_re * lam_re + lam_im * lam_im
    k_re = ((lb_re - 1.0) * lam_re + lb_im * lam_im) / den
    k_im = (lb_im * lam_re - (lb_re - 1.0) * lam_im) / den
    bb_re = k_re[..., None] * b_re - k_im[..., None] * b_im
    bb_im = k_re[..., None] * b_im + k_im[..., None] * b_re

    g, p, hg = b_re.shape
    gs = g // SSM_SLABS
    eye = jnp.eye(gs, dtype=F32)

    def b_slabs(x):
        x = x.reshape(SSM_SLABS, gs, p, hg)
        return jnp.einsum('sgph,gk->sghkp', x, eye).reshape(SSM_SLABS, gs * hg, gs * p).astype(BF16)

    def c_slabs(x):
        x = x.reshape(SSM_SLABS, gs, hg, p)
        return jnp.einsum('sghp,gk->sgpkh', x, eye).reshape(SSM_SLABS, gs * p, gs * hg).astype(BF16)

    lam = jnp.stack(lam_pow(1.0))
    lam_sub = jnp.concatenate([jnp.stack(lam_pow(float(SSM_SUB * d))) for d in (1, 2, 4)])
    sub_pows = [lam_pow(float(SSM_SUB * (j + 1))) for j in range(SUBLANES)]
    pow_sub = jnp.stack([x[0] for x in sub_pows] + [x[1] for x in sub_pows])
    step_pows = [lam_pow(float(i + 1)) for i in range(SSM_SUB)]
    pow_step = jnp.stack([x[0] for x in step_pows] + [x[1] for x in step_pows])
    return b_slabs(bb_re), b_slabs(bb_im), c_slabs(c_re), c_slabs(c_im), lam, lam_sub, pow_sub, pow_step


def _ssm_permutation():
    perm = np.zeros((SSM_CHUNK, SSM_CHUNK), np.float32)
    for j in range(SUBLANES):
        for i in range(SSM_SUB):
            perm[i * SUBLANES + j, j * SSM_SUB + i] = 1.0
    return jnp.asarray(perm, BF16), jnp.asarray(perm.T, BF16)


def _ssm(z, tables, d_skip, w_glu, b_glu):
    s = z.shape[0]
    w = BRANCH_WIDTH
    perm, perm_t = _ssm_permutation()
    b_re, b_im, c_re, c_im, lam, lam_sub, pow_sub, pow_step = tables
    full = lambda a: pl.BlockSpec(a.shape, lambda c, _n=a.ndim: (0,) * _n)
    operands = (perm, perm_t, b_re, b_im, c_re, c_im, lam, lam_sub, pow_sub, pow_step,
                d_skip.reshape(1, w), w_glu, b_glu.reshape(1, w))
    return pl.pallas_call(
        _ssm_kernel,
        out_shape=jax.ShapeDtypeStruct((s, w), BF16),
        grid=(s // SSM_CHUNK,),
        in_specs=[pl.BlockSpec((SSM_CHUNK, w), lambda c: (c, Z_SSM // w))] + [full(a) for a in operands],
        out_specs=pl.BlockSpec((SSM_CHUNK, w), lambda c: (c, 0)),
        scratch_shapes=[
            pltpu.VMEM((SSM_CHUNK, SSM_STATES), F32),
            pltpu.VMEM((SSM_CHUNK, SSM_STATES), F32),
            pltpu.VMEM((2 * SUBLANES, SSM_STATES), F32),
            pltpu.VMEM((2, SSM_STATES), F32),
        ],
        compiler_params=_params(("arbitrary",)),
        name="s5_ssm",
    )(z, *operands)


def _conv_kernel(val_ref, gate_ref, hval_ref, hgate_ref, w_ref, b_ref, lng_ref, lnb_ref, o_ref,
                 a_ref, y_ref, *, tt, row_chunk, lane_chunk):
    a_ref[CONV_HALO:CONV_HALO + tt, :] = val_ref[...].astype(F32) * _sigmoid(gate_ref[...].astype(F32))
    halo = hval_ref[...].astype(F32) * _sigmoid(hgate_ref[...].astype(F32))
    a_ref[0:CONV_HALO, :] = jnp.where(pl.program_id(0) > 0, halo, 0.0)

    first = CONV_HALO - (CONV_WIDTH - 1)
    for c0 in range(0, BRANCH_WIDTH, lane_chunk):
        lanes = slice(c0, c0 + lane_chunk)
        for r0 in range(0, tt, row_chunk):
            acc = jnp.broadcast_to(b_ref[:, lanes], (row_chunk, lane_chunk))
            for j in range(CONV_WIDTH):
                acc = acc + w_ref[j:j + 1, lanes] * a_ref[r0 + first + j:r0 + first + j + row_chunk, lanes]
            y_ref[r0:r0 + row_chunk, lanes] = acc

    y = y_ref[...]
    mu = jnp.mean(y, axis=-1, keepdims=True)
    yc = y - mu
    var = jnp.mean(yc * yc, axis=-1, keepdims=True)
    yn = yc * lax.rsqrt(var + EPS) * lng_ref[...] + lnb_ref[...]
    o_ref[...] = (yn * _sigmoid(yn)).astype(BF16)


def _conv(z, w_dw, b_dw, ln_g, ln_b, *, tt=256):
    s = z.shape[0]
    w = BRANCH_WIDTH
    hb = tt // CONV_HALO
    halo_map = lambda col: (lambda i: (jnp.maximum(i * hb - 1, 0), col))
    return pl.pallas_call(
        functools.partial(_conv_kernel, tt=tt, row_chunk=64, lane_chunk=256),
        out_shape=jax.ShapeDtypeStruct((s, w), BF16),
        grid=(s // tt,),
        in_specs=[
            pl.BlockSpec((tt, w), lambda i: (i, Z_CVAL // w)),
            pl.BlockSpec((tt, w), lambda i: (i, Z_CGATE // w)),
            pl.BlockSpec((CONV_HALO, w), halo_map(Z_CVAL // w)),
            pl.BlockSpec((CONV_HALO, w), halo_map(Z_CGATE // w)),
            pl.BlockSpec((CONV_WIDTH, w), lambda i: (0, 0)),
            pl.BlockSpec((1, w), lambda i: (0, 0)),
            pl.BlockSpec((1, w), lambda i: (0, 0)),
            pl.BlockSpec((1, w), lambda i: (0, 0)),
        ],
        out_specs=pl.BlockSpec((tt, w), lambda i: (i, 0)),
        scratch_shapes=[pltpu.VMEM((CONV_HALO + tt, w), F32), pltpu.VMEM((tt, w), F32)],
        compiler_params=_params(("parallel",)),
        name="conformer_conv",
    )(z, z, z, z, w_dw, b_dw.reshape(1, w), ln_g.reshape(1, w), ln_b.reshape(1, w))


def _pool_kernel(p_ref, hp_ref, w_ref, b_ref, sc_ref, o_ref, e_ref, *, tt):
    i = pl.program_id(0)
    e_ref[POOL_HALO:POOL_HALO + tt, :] = p_ref[...].astype(F32)
    e_ref[0:POOL_HALO, :] = jnp.where(i > 0, hp_ref[...].astype(F32), 0.0)
    pos = (i * tt + 1 + lax.broadcasted_iota(jnp.int32, (tt, 1), 0)).astype(F32)
    for g, win in enumerate(POOL_WINDOWS):
        lanes = slice(g * POOL_GROUP, (g + 1) * POOL_GROUP)
        cur = e_ref[POOL_HALO:POOL_HALO + tt, lanes]
        tot = cur
        for lag in range(1, win):
            tot = tot + e_ref[POOL_HALO - lag:POOL_HALO - lag + tt, lanes]
        zg = tot / jnp.minimum(pos, float(win)) - cur
        out = jnp.dot(zg.astype(BF16), w_ref[g], preferred_element_type=F32) + b_ref[g:g + 1, :]
        o_ref[:, lanes] = (out * sc_ref[:, lanes]).astype(BF16)


def _pool(z, w_pool, b_pool, scale, *, tt=512):
    s = z.shape[0]
    w = BRANCH_WIDTH
    hb = tt // POOL_HALO
    return pl.pallas_call(
        functools.partial(_pool_kernel, tt=tt),
        out_shape=jax.ShapeDtypeStruct((s, w), BF16),
        grid=(s // tt,),
        in_specs=[
            pl.BlockSpec((tt, w), lambda i: (i, Z_POOL // w)),
            pl.BlockSpec((POOL_HALO, w), lambda i: (jnp.maximum(i * hb - 1, 0), Z_POOL // w)),
            pl.BlockSpec(w_pool.shape, lambda i: (0, 0, 0)),
            pl.BlockSpec(b_pool.shape, lambda i: (0, 0)),
            pl.BlockSpec((1, w), lambda i: (0, 0)),
        ],
        out_specs=pl.BlockSpec((tt, w), lambda i: (i, 0)),
        scratch_shapes=[pltpu.VMEM((POOL_HALO + tt, w), F32)],
        compiler_params=_params(("parallel",)),
        name="multiscale_pool",
    )(z, z, w_pool, b_pool, scale.reshape(1, w))


def _merge_kernel(u_ref, ya_ref, yb_ref, yc_ref, yd_ref, gw_ref, gb_ref, up_ref, o_ref):
    u = u_ref[...]
    acc = None
    for n, y_ref in enumerate((ya_ref, yb_ref, yc_ref, yd_ref)):
        gate = _sigmoid(jnp.dot(u, gw_ref[n], preferred_element_type=F32) + gb_ref[n:n + 1, :])
        term = gate * jnp.dot(y_ref[...], up_ref[n], preferred_element_type=F32)
        acc = term if acc is None else acc + term
    o_ref[...] = acc.astype(BF16)


def _merge(u, ys, gate_w, gate_b, up_w, *, tm=1024):
    s, d = u.shape
    nb, w = up_w.shape[0], up_w.shape[1]
    gb = GATE_BLOCK
    y_spec = pl.BlockSpec((tm, w), lambda i, k: (i, 0))
    return pl.pallas_call(
        _merge_kernel,
        out_shape=jax.ShapeDtypeStruct((s, d), BF16),
        grid=(s // tm, d // gb),
        in_specs=[
            pl.BlockSpec((tm, gb), lambda i, k: (i, k)),
            y_spec, y_spec, y_spec, y_spec,
            pl.BlockSpec((nb, None, gb, gb), lambda i, k: (0, k, 0, 0)),
            pl.BlockSpec((nb, gb), lambda i, k: (0, k)),
            pl.BlockSpec((nb, w, gb), lambda i, k: (0, 0, k)),
        ],
        out_specs=pl.BlockSpec((tm, gb), lambda i, k: (i, k)),
        compiler_params=_params(("parallel", "arbitrary")),
        name="gated_merge",
    )(u, *ys, gate_w, gate_b, up_w)


def _ffn_block(h, pre_g, w_gate, w_up, w_down, post_g):
    act = _ffn_up(h, pre_g, w_gate.astype(BF16), w_up.astype(BF16))
    return _proj_norm_res(act, w_down.astype(BF16), post_g, h, 0.5)


def _mixer_block(h, pre_g, w_in, forget_bias, ssm_params, ssm_d, ssm_w_glu, ssm_b_glu,
                 conv_w, conv_b, conv_ln_g, conv_ln_b, pool_w, pool_b, pool_scale,
                 branch_w_up, gate_w, gate_b, w_out, post_g):
    off_f = 3 * BRANCH_WIDTH
    w_q = w_in[:, :BRANCH_WIDTH] * (ATT_HEAD_DIM ** -0.5)
    w_main = jnp.concatenate([w_q, w_in[:, BRANCH_WIDTH:off_f], w_in[:, off_f + ATT_HEADS:]], axis=1).astype(BF16)
    w_f = jnp.zeros((HEAD_ROWS, D_MODEL), F32).at[:ATT_HEADS].set(w_in[:, off_f:off_f + ATT_HEADS].T).astype(BF16)
    b_f = jnp.zeros((HEAD_ROWS, 1), F32).at[:ATT_HEADS, 0].set(forget_bias)

    z, u = _in_proj(h, pre_g, w_main)
    c = _forget_cumsum(u, w_f, b_f)
    y_a = _attention(z, c)
    y_b = _ssm(z, _ssm_tables(*ssm_params), ssm_d, ssm_w_glu.astype(BF16), ssm_b_glu)
    y_c = _conv(z, conv_w, conv_b, conv_ln_g, conv_ln_b)
    y_d = _pool(z, pool_w.astype(BF16), pool_b, pool_scale)
    merged = _merge(u, (y_a, y_b, y_c, y_d), gate_w.astype(BF16), gate_b, branch_w_up.astype(BF16))
    return _proj_norm_res(merged, w_out.astype(BF16), post_g, h, 1.0)


def kernel(x, ffn1_pre_g, ffn1_w_gate, ffn1_w_up, ffn1_w_down, ffn1_post_g, mix_pre_g, w_in, forget_bias, ssm_lam_re, ssm_lam_im, ssm_log_dt, ssm_b_re, ssm_b_im, ssm_c_re, ssm_c_im, ssm_d, ssm_w_glu, ssm_b_glu, conv_w, conv_b, conv_ln_g, conv_ln_b, pool_w, pool_b, pool_scale, branch_w_up, gate_w, gate_b, w_out, mix_post_g, ffn2_pre_g, ffn2_w_gate, ffn2_w_up, ffn2_w_down, ffn2_post_g):
    batch, seq, d = x.shape
    outs = []
    for b in range(batch):
        h = x[b]
        for l in range(ffn1_pre_g.shape[0]):
            h = _ffn_block(h, ffn1_pre_g[l], ffn1_w_gate[l], ffn1_w_up[l], ffn1_w_down[l], ffn1_post_g[l])
            h = _mixer_block(
                h, mix_pre_g[l], w_in[l], forget_bias[l],
                (ssm_lam_re[l], ssm_lam_im[l], ssm_log_dt[l], ssm_b_re[l], ssm_b_im[l], ssm_c_re[l], ssm_c_im[l]),
                ssm_d[l], ssm_w_glu[l], ssm_b_glu[l], conv_w[l], conv_b[l], conv_ln_g[l], conv_ln_b[l],
                pool_w[l], pool_b[l], pool_scale[l], branch_w_up[l], gate_w[l], gate_b[l], w_out[l], mix_post_g[l])
            h = _ffn_block(h, ffn2_pre_g[l], ffn2_w_gate[l], ffn2_w_up[l], ffn2_w_down[l], ffn2_post_g[l])
        outs.append(h)
    return jnp.stack(outs)
```

```python
import functools
import math

import numpy as np
import jax
import jax.numpy as jnp
from jax import lax
from jax.experimental import pallas as pl
from jax.experimental.pallas import tpu as pltpu

F32 = jnp.float32
BF16 = jnp.bfloat16

D_MODEL = 4096
BRANCH_WIDTH = 1024
ATT_HEADS = 8
ATT_HEAD_DIM = 128
SSM_GROUP = 16
SSM_GROUPS = 64
SSM_STATE = 64
CONV_WIDTH = 31
POOL_WINDOWS = (2, 4, 8, 16)
POOL_GROUP = 256
GATE_BLOCKS = 16
GATE_BLOCK = 256
D_FF = 6144
EPS = 1e-6

V7X_VMEM_BYTES = 64 * 1024 * 1024
VMEM_LIMIT_BYTES = V7X_VMEM_BYTES - 8 * 1024 * 1024
SUBLANES = 8
LANES = 128

Z_Q, Z_K, Z_V, Z_SSM, Z_CVAL, Z_CGATE, Z_POOL = (n * BRANCH_WIDTH for n in range(7))
Z_WIDTH = 7 * BRANCH_WIDTH
HEAD_ROWS = 16

SSM_CHUNK = 256
SSM_SUB = SSM_CHUNK // SUBLANES
SSM_SLABS = 4
SSM_SLAB_IN = BRANCH_WIDTH // SSM_SLABS
SSM_STATES = SSM_GROUPS * SSM_STATE
SSM_SLAB_STATE = SSM_STATES // SSM_SLABS
SSM_STRIP = 512

CONV_HALO = 32
POOL_HALO = 16

NEG_BIG = -1e30


def _params(semantics):
    return pltpu.CompilerParams(dimension_semantics=semantics, vmem_limit_bytes=VMEM_LIMIT_BYTES)


def _sigmoid(x):
    return 1.0 / (1.0 + jnp.exp(-x))


def _rms_scale(x):
    return lax.rsqrt(jnp.mean(x * x, axis=-1, keepdims=True) + EPS)


def _ffn_up_kernel(h_ref, g_ref, wg_ref, wu_ref, o_ref, xn_ref):
    @pl.when(pl.program_id(1) == 0)
    def _():
        x = h_ref[...]
        xn_ref[...] = (x * _rms_scale(x) * g_ref[...]).astype(BF16)

    xn = xn_ref[...]
    a = jnp.dot(xn, wg_ref[...], preferred_element_type=F32)
    b = jnp.dot(xn, wu_ref[...], preferred_element_type=F32)
    o_ref[...] = (a * _sigmoid(a) * b).astype(BF16)


def _ffn_up(h, g, wg, wu, *, tm=512, tn=512):
    s, d = h.shape
    f = wg.shape[1]
    return pl.pallas_call(
        _ffn_up_kernel,
        out_shape=jax.ShapeDtypeStruct((s, f), BF16),
        grid=(s // tm, f // tn),
        in_specs=[
            pl.BlockSpec((tm, d), lambda i, j: (i, 0)),
            pl.BlockSpec((1, d), lambda i, j: (0, 0)),
            pl.BlockSpec((d, tn), lambda i, j: (0, j)),
            pl.BlockSpec((d, tn), lambda i, j: (0, j)),
        ],
        out_specs=pl.BlockSpec((tm, tn), lambda i, j: (i, j)),
        scratch_shapes=[pltpu.VMEM((tm, d), BF16)],
        compiler_params=_params(("parallel", "arbitrary")),
        name="ffn_up",
    )(h, g.reshape(1, d), wg, wu)


def _proj_norm_res_kernel(a_ref, w_ref, g_ref, h_ref, o_ref, *, alpha, col_chunk, row_chunk):
    k = pl.program_id(1)
    tm, d = o_ref.shape

    @pl.when(k == 0)
    def _():
        o_ref[...] = jnp.zeros_like(o_ref)

    a = a_ref[...]
    for c0 in range(0, d, col_chunk):
        cols = slice(c0, c0 + col_chunk)
        o_ref[:, cols] += jnp.dot(a, w_ref[:, cols], preferred_element_type=F32)

    @pl.when(k == pl.num_programs(1) - 1)
    def _():
        for r0 in range(0, tm, row_chunk):
            rows = slice(r0, r0 + row_chunk)
            acc = o_ref[rows, :]
            o_ref[rows, :] = h_ref[rows, :] + alpha * (acc * _rms_scale(acc) * g_ref[...])


def _proj_norm_res(a, w, g, h, alpha, *, tm=512, tk=512):
    s, kdim = a.shape
    d = w.shape[1]
    return pl.pallas_call(
        functools.partial(_proj_norm_res_kernel, alpha=alpha, col_chunk=1024, row_chunk=128),
        out_shape=jax.ShapeDtypeStruct((s, d), F32),
        grid=(s // tm, kdim // tk),
        in_specs=[
            pl.BlockSpec((tm, tk), lambda i, k: (i, k)),
            pl.BlockSpec((tk, d), lambda i, k: (k, 0)),
            pl.BlockSpec((1, d), lambda i, k: (0, 0)),
            pl.BlockSpec((tm, d), lambda i, k: (i, 0)),
        ],
        out_specs=pl.BlockSpec((tm, d), lambda i, k: (i, 0)),
        compiler_params=_params(("parallel", "arbitrary")),
        name="proj_norm_res",
    )(a, w, g.reshape(1, d), h)


def _in_proj_kernel(h_ref, g_ref, w_ref, z_ref, u_ref):
    @pl.when(pl.program_id(1) == 0)
    def _():
        x = h_ref[...]
        u_ref[...] = (x * _rms_scale(x) * g_ref[...]).astype(BF16)

    z_ref[...] = jnp.dot(u_ref[...], w_ref[...], preferred_element_type=F32).astype(BF16)


def _in_proj(h, g, w, *, tm=512, tn=512):
    s, d = h.shape
    n = w.shape[1]
    return pl.pallas_call(
        _in_proj_kernel,
        out_shape=(jax.ShapeDtypeStruct((s, n), BF16), jax.ShapeDtypeStruct((s, d), BF16)),
        grid=(s // tm, n // tn),
        in_specs=[
            pl.BlockSpec((tm, d), lambda i, j: (i, 0)),
            pl.BlockSpec((1, d), lambda i, j: (0, 0)),
            pl.BlockSpec((d, tn), lambda i, j: (0, j)),
        ],
        out_specs=(
            pl.BlockSpec((tm, tn), lambda i, j: (i, j)),
            pl.BlockSpec((tm, d), lambda i, j: (i, 0)),
        ),
        compiler_params=_params(("parallel", "arbitrary")),
        name="in_proj",
    )(h, g.reshape(1, d), w)


def _split_bf16(x, parts=3):
    out = []
    for _ in range(parts):
        p = x.astype(BF16)
        out.append(p)
        x = x - p.astype(F32)
    return out


def _forget_kernel(u_ref, wf_ref, bf_ref, tri_ref, c_ref, carry_ref):
    @pl.when(pl.program_id(0) == 0)
    def _():
        carry_ref[...] = jnp.zeros_like(carry_ref)

    f = lax.dot_general(wf_ref[...], u_ref[...], (((1,), (1,)), ((), ())),
                        preferred_element_type=F32)
    x = f + bf_ref[...]
    log_f = jnp.minimum(x, 0.0) - jnp.log(1.0 + jnp.exp(-jnp.abs(x)))
    tri = tri_ref[...]
    cs = sum(jnp.dot(p, tri, preferred_element_type=F32) for p in _split_bf16(log_f))
    c = cs + carry_ref[...]
    c_ref[...] = c
    carry_ref[...] = c[:, c.shape[1] - 1:]


def _forget_cumsum(u, wf_t, bf, *, tm=512):
    s, d = u.shape
    tri = jnp.asarray(np.triu(np.ones((tm, tm), np.float32)), BF16)
    return pl.pallas_call(
        _forget_kernel,
        out_shape=jax.ShapeDtypeStruct((HEAD_ROWS, s), F32),
        grid=(s // tm,),
        in_specs=[
            pl.BlockSpec((tm, d), lambda i: (i, 0)),
            pl.BlockSpec((HEAD_ROWS, d), lambda i: (0, 0)),
            pl.BlockSpec((HEAD_ROWS, 1), lambda i: (0, 0)),
            pl.BlockSpec((tm, tm), lambda i: (0, 0)),
        ],
        out_specs=pl.BlockSpec((HEAD_ROWS, tm), lambda i: (0, i)),
        scratch_shapes=[pltpu.VMEM((HEAD_ROWS, 1), F32)],
        compiler_params=_params(("arbitrary",)),
        name="forget_cumsum",
    )(u, wf_t, bf, tri)


def _lane_groups(x):
    return [x[:, g * LANES:(g + 1) * LANES] for g in range(x.shape[1] // LANES)]


def _attn_kernel(q_ref, k_ref, v_ref, c_ref, o_ref, vaug_ref, m_ref, acc_ref, *, tq):
    h = pl.program_id(0)
    i = pl.program_id(1)
    dh = ATT_HEAD_DIM

    @pl.when(i == 0)
    def _():
        vaug_ref[:, 0:dh] = v_ref[...]
        vaug_ref[:, dh:2 * dh] = jnp.ones((v_ref.shape[0], dh), BF16)

    q = q_ref[...]
    start = pl.multiple_of(i * tq, tq)
    c_q = c_ref[pl.ds(h, 1), pl.ds(start, tq)]
    c_base = c_q[:, 0:1]
    nt_dims = (((1,), (1,)), ((), ()))

    def scores(off, width):
        c_k = c_ref[pl.ds(h, 1), pl.ds(off, width)] - c_base
        return lax.dot_general(q, k_ref[pl.ds(off, width), :], nt_dims, preferred_element_type=F32) - c_k

    def diag_scores():
        rows = lax.broadcasted_iota(jnp.int32, (tq, tq), 0)
        cols = lax.broadcasted_iota(jnp.int32, (tq, tq), 1)
        return jnp.where(cols <= rows, scores(start, tq), NEG_BIG)

    def lane_max(s):
        return functools.reduce(jnp.maximum, _lane_groups(s))

    def probs(s):
        m_rep = m_ref[...]
        return jnp.concatenate([jnp.exp(g - m_rep) for g in _lane_groups(s)], axis=1).astype(BF16)

    def update_max(off, width):
        m_ref[...] = jnp.maximum(m_ref[...], lane_max(scores(off, width)))

    def accumulate(off, width):
        acc_ref[...] += jnp.dot(probs(scores(off, width)), vaug_ref[pl.ds(off, width), :],
                                preferred_element_type=F32)

    def over_past_keys(fn):
        def body(jj, carry):
            fn(pl.multiple_of(jj * (2 * tq), 2 * tq), 2 * tq)
            return carry

        lax.fori_loop(0, i // 2, body, 0)

        @pl.when(i % 2 == 1)
        def _():
            fn(pl.multiple_of((i - 1) * tq, tq), tq)

    m_ref[...] = lane_max(diag_scores())
    over_past_keys(update_max)
    m_ref[...] = jnp.broadcast_to(jnp.max(m_ref[...], axis=1, keepdims=True), m_ref.shape)

    acc_ref[...] = jnp.dot(probs(diag_scores()), vaug_ref[pl.ds(start, tq), :], preferred_element_type=F32)
    over_past_keys(accumulate)
    o_ref[...] = (acc_ref[:, 0:dh] / acc_ref[:, dh:dh + 1]).astype(BF16)


def _attention(z, c, *, tq=512):
    s = z.shape[0]
    dh = ATT_HEAD_DIM
    return pl.pallas_call(
        functools.partial(_attn_kernel, tq=tq),
        out_shape=jax.ShapeDtypeStruct((s, BRANCH_WIDTH), BF16),
        grid=(ATT_HEADS, s // tq),
        in_specs=[
            pl.BlockSpec((tq, dh), lambda h, i: (i, Z_Q // dh + h)),
            pl.BlockSpec((s, dh), lambda h, i: (0, Z_K // dh + h)),
            pl.BlockSpec((s, dh), lambda h, i: (0, Z_V // dh + h)),
            pl.BlockSpec((HEAD_ROWS, s), lambda h, i: (0, 0)),
        ],
        out_specs=pl.BlockSpec((tq, dh), lambda h, i: (i, h)),
        scratch_shapes=[pltpu.VMEM((s, 2 * dh), BF16), pltpu.VMEM((tq, LANES), F32),
                        pltpu.VMEM((tq, 2 * dh), F32)],
        compiler_params=_params(("parallel", "arbitrary")),
        name="forget_attention",
    )(z, z, z, c)


def _cmul(ar, ai, br, bi):
    return ar * br - ai * bi, ar * bi + ai * br


def _gelu_tanh(x):
    return 0.5 * x * (1.0 + jnp.tanh(math.sqrt(2.0 / math.pi) * (x + 0.044715 * (x * x * x))))


def _ssm_kernel(u_ref, perm_ref, perm_t_ref, b_re_ref, b_im_ref, c_re_ref, c_im_ref,
                lam_ref, lam_sub_ref, pow_sub_ref, pow_step_ref, d_ref, wglu_ref, bglu_ref,
                o_ref, x_re_ref, x_im_ref, cin_ref, carry_ref):
    @pl.when(pl.program_id(0) == 0)
    def _():
        carry_ref[...] = jnp.zeros_like(carry_ref)

    n_sub = SSM_SUB
    n_strips = SSM_STATES // SSM_STRIP
    u_p = jnp.dot(perm_ref[...], u_ref[...], preferred_element_type=F32).astype(BF16)

    for sl in range(SSM_SLABS):
        u_sl = u_p[:, sl * SSM_SLAB_IN:(sl + 1) * SSM_SLAB_IN]
        cols = slice(sl * SSM_SLAB_STATE, (sl + 1) * SSM_SLAB_STATE)
        x_re_ref[:, cols] = jnp.dot(u_sl, b_re_ref[sl], preferred_element_type=F32)
        x_im_ref[:, cols] = jnp.dot(u_sl, b_im_ref[sl], preferred_element_type=F32)

    def scan_strip(st, carry):
        lanes = pl.ds(pl.multiple_of(st * SSM_STRIP, SSM_STRIP), SSM_STRIP)
        lr = jnp.broadcast_to(lam_ref[0:1, lanes], (SUBLANES, SSM_STRIP))
        li = jnp.broadcast_to(lam_ref[1:2, lanes], (SUBLANES, SSM_STRIP))
        sr = x_re_ref[0:SUBLANES, lanes]
        si = x_im_ref[0:SUBLANES, lanes]
        for i in range(1, n_sub):
            rows = slice(i * SUBLANES, (i + 1) * SUBLANES)
            pr, pi = _cmul(lr, li, sr, si)
            sr = pr + x_re_ref[rows, lanes]
            si = pi + x_im_ref[rows, lanes]
            x_re_ref[rows, lanes] = sr
            x_im_ref[rows, lanes] = si
        return carry

    lax.fori_loop(0, n_strips, scan_strip, 0)

    last = slice((n_sub - 1) * SUBLANES, n_sub * SUBLANES)
    tr = x_re_ref[last, :]
    ti = x_im_ref[last, :]
    row = lax.broadcasted_iota(jnp.int32, (SUBLANES, SSM_STATES), 0)
    for lvl, dist in enumerate((1, 2, 4)):
        ar = lam_sub_ref[2 * lvl:2 * lvl + 1, :]
        ai = lam_sub_ref[2 * lvl + 1:2 * lvl + 2, :]
        shr = jnp.where(row >= dist, pltpu.roll(tr, dist, 0), 0.0)
        shi = jnp.where(row >= dist, pltpu.roll(ti, dist, 0), 0.0)
        pr, pi = _cmul(ar, ai, shr, shi)
        tr = tr + pr
        ti = ti + pi
    in_r = carry_ref[0:1, :]
    in_i = carry_ref[1:2, :]
    pr, pi = _cmul(pow_sub_ref[0:SUBLANES, :], pow_sub_ref[SUBLANES:2 * SUBLANES, :], in_r, in_i)
    tr = tr + pr
    ti = ti + pi
    cin_ref[0:SUBLANES, :] = jnp.where(row >= 1, pltpu.roll(tr, 1, 0), in_r)
    cin_ref[SUBLANES:2 * SUBLANES, :] = jnp.where(row >= 1, pltpu.roll(ti, 1, 0), in_i)
    carry_ref[0:1, :] = tr[SUBLANES - 1:SUBLANES, :]
    carry_ref[1:2, :] = ti[SUBLANES - 1:SUBLANES, :]

    def fix_strip(st, carry):
        lanes = pl.ds(pl.multiple_of(st * SSM_STRIP, SSM_STRIP), SSM_STRIP)
        cr = cin_ref[0:SUBLANES, lanes]
        ci = cin_ref[SUBLANES:2 * SUBLANES, lanes]
        for i in range(n_sub):
            rows = slice(i * SUBLANES, (i + 1) * SUBLANES)
            pr, pi = _cmul(pow_step_ref[i:i + 1, lanes], pow_step_ref[n_sub + i:n_sub + i + 1, lanes], cr, ci)
            x_re_ref[rows, lanes] = x_re_ref[rows, lanes] + pr
            x_im_ref[rows, lanes] = x_im_ref[rows, lanes] + pi
        return carry

    lax.fori_loop(0, n_strips, fix_strip, 0)

    ys = []
    for sl in range(SSM_SLABS):
        cols = slice(sl * SSM_SLAB_STATE, (sl + 1) * SSM_SLAB_STATE)
        ys.append(jnp.dot(x_re_ref[:, cols].astype(BF16), c_re_ref[sl], preferred_element_type=F32)
                  - jnp.dot(x_im_ref[:, cols].astype(BF16), c_im_ref[sl], preferred_element_type=F32))
    y = jnp.concatenate(ys, axis=1) + d_ref[...] * u_p.astype(F32)
    y = _gelu_tanh(y)
    gate = jnp.dot(y.astype(BF16), wglu_ref[...], preferred_element_type=F32) + bglu_ref[...]
    y = (y * _sigmoid(gate)).astype(BF16)
    o_ref[...] = jnp.dot(perm_t_ref[...], y, preferred_element_type=F32).astype(BF16)


def _ssm_tables(lam_re, lam_im, log_dt, b_re, b_im, c_re, c_im):
    dt = jnp.exp(log_dt)[:, None]
    a_re = lam_re * dt
    a_im = lam_im * dt

    def lam_pow(n):
        mag = jnp.exp(a_re * n)
        return (mag * jnp.cos(a_im * n)).reshape(-1), (mag * jnp.sin(a_im * n)).reshape(-1)

    lb_re = jnp.exp(a_re) * jnp.cos(a_im)
    lb_im = jnp.exp(a_re) * jnp.sin(a_im)
    den = lam_re * lam_re + lam_im * lam_im
    k_re = ((lb_re - 1.0) * lam_re + lb_im * lam_im) / den
    k_im = (lb_im * lam_re - (lb_re - 1.0) * lam_im) / den
    bb_re = k_re[..., None] * b_re - k_im[..., None] * b_im
    bb_im = k_re[..., None] * b_im + k_im[..., None] * b_re

    g, p, hg = b_re.shape
    gs = g // SSM_SLABS
    eye = jnp.eye(gs, dtype=F32)

    def b_slabs(x):
        x = x.reshape(SSM_SLABS, gs, p, hg)
        return jnp.einsum('sgph,gk->sghkp', x, eye).reshape(SSM_SLABS, gs * hg, gs * p).astype(BF16)

    def c_slabs(x):
        x = x.reshape(SSM_SLABS, gs, hg, p)
        return jnp.einsum('sghp,gk->sgpkh', x, eye).reshape(SSM_SLABS, gs * p, gs * hg).astype(BF16)

    lam = jnp.stack(lam_pow(1.0))
    lam_sub = jnp.concatenate([jnp.stack(lam_pow(float(SSM_SUB * d))) for d in (1, 2, 4)])
    sub_pows = [lam_pow(float(SSM_SUB * (j + 1))) for j in range(SUBLANES)]
    pow_sub = jnp.stack([x[0] for x in sub_pows] + [x[1] for x in sub_pows])
    step_pows = [lam_pow(float(i + 1)) for i in range(SSM_SUB)]
    pow_step = jnp.stack([x[0] for x in step_pows] + [x[1] for x in step_pows])
    return b_slabs(bb_re), b_slabs(bb_im), c_slabs(c_re), c_slabs(c_im), lam, lam_sub, pow_sub, pow_step


def _ssm_permutation():
    perm = np.zeros((SSM_CHUNK, SSM_CHUNK), np.float32)
    for j in range(SUBLANES):
        for i in range(SSM_SUB):
            perm[i * SUBLANES + j, j * SSM_SUB + i] = 1.0
    return jnp.asarray(perm, BF16), jnp.asarray(perm.T, BF16)


def _ssm(z, tables, d_skip, w_glu, b_glu):
    s = z.shape[0]
    w = BRANCH_WIDTH
    perm, perm_t = _ssm_permutation()
    b_re, b_im, c_re, c_im, lam, lam_sub, pow_sub, pow_step = tables
    full = lambda a: pl.BlockSpec(a.shape, lambda c, _n=a.ndim: (0,) * _n)
    operands = (perm, perm_t, b_re, b_im, c_re, c_im, lam, lam_sub, pow_sub, pow_step,
                d_skip.reshape(1, w), w_glu, b_glu.reshape(1, w))
    return pl.pallas_call(
        _ssm_kernel,
        out_shape=jax.ShapeDtypeStruct((s, w), BF16),
        grid=(s // SSM_CHUNK,),
        in_specs=[pl.BlockSpec((SSM_CHUNK, w), lambda c: (c, Z_SSM // w))] + [full(a) for a in operands],
        out_specs=pl.BlockSpec((SSM_CHUNK, w), lambda c: (c, 0)),
        scratch_shapes=[
            pltpu.VMEM((SSM_CHUNK, SSM_STATES), F32),
            pltpu.VMEM((SSM_CHUNK, SSM_STATES), F32),
            pltpu.VMEM((2 * SUBLANES, SSM_STATES), F32),
            pltpu.VMEM((2, SSM_STATES), F32),
        ],
        compiler_params=_params(("arbitrary",)),
        name="s5_ssm",
    )(z, *operands)


def _conv_kernel(val_ref, gate_ref, hval_ref, hgate_ref, w_ref, b_ref, lng_ref, lnb_ref, o_ref,
                 a_ref, y_ref, *, tt, row_chunk, lane_chunk):
    a_ref[CONV_HALO:CONV_HALO + tt, :] = val_ref[...].astype(F32) * _sigmoid(gate_ref[...].astype(F32))
    halo = hval_ref[...].astype(F32) * _sigmoid(hgate_ref[...].astype(F32))
    a_ref[0:CONV_HALO, :] = jnp.where(pl.program_id(0) > 0, halo, 0.0)

    first = CONV_HALO - (CONV_WIDTH - 1)
    for c0 in range(0, BRANCH_WIDTH, lane_chunk):
        lanes = slice(c0, c0 + lane_chunk)
        for r0 in range(0, tt, row_chunk):
            acc = jnp.broadcast_to(b_ref[:, lanes], (row_chunk, lane_chunk))
            for j in range(CONV_WIDTH):
                acc = acc + w_ref[j:j + 1, lanes] * a_ref[r0 + first + j:r0 + first + j + row_chunk, lanes]
            y_ref[r0:r0 + row_chunk, lanes] = acc

    y = y_ref[...]
    mu = jnp.mean(y, axis=-1, keepdims=True)
    yc = y - mu
    var = jnp.mean(yc * yc, axis=-1, keepdims=True)
    yn = yc * lax.rsqrt(var + EPS) * lng_ref[...] + lnb_ref[...]
    o_ref[...] = (yn * _sigmoid(yn)).astype(BF16)


def _conv(z, w_dw, b_dw, ln_g, ln_b, *, tt=256):
    s = z.shape[0]
    w = BRANCH_WIDTH
    hb = tt // CONV_HALO
    halo_map = lambda col: (lambda i: (jnp.maximum(i * hb - 1, 0), col))
    return pl.pallas_call(
        functools.partial(_conv_kernel, tt=tt, row_chunk=64, lane_chunk=256),
        out_shape=jax.ShapeDtypeStruct((s, w), BF16),
        grid=(s // tt,),
        in_specs=[
            pl.BlockSpec((tt, w), lambda i: (i, Z_CVAL // w)),
            pl.BlockSpec((tt, w), lambda i: (i, Z_CGATE // w)),
            pl.BlockSpec((CONV_HALO, w), halo_map(Z_CVAL // w)),
            pl.BlockSpec((CONV_HALO, w), halo_map(Z_CGATE // w)),
            pl.BlockSpec((CONV_WIDTH, w), lambda i: (0, 0)),
            pl.BlockSpec((1, w), lambda i: (0, 0)),
            pl.BlockSpec((1, w), lambda i: (0, 0)),
            pl.BlockSpec((1, w), lambda i: (0, 0)),
        ],
        out_specs=pl.BlockSpec((tt, w), lambda i: (i, 0)),
        scratch_shapes=[pltpu.VMEM((CONV_HALO + tt, w), F32), pltpu.VMEM((tt, w), F32)],
        compiler_params=_params(("parallel",)),
        name="conformer_conv",
    )(z, z, z, z, w_dw, b_dw.reshape(1, w), ln_g.reshape(1, w), ln_b.reshape(1, w))


def _pool_kernel(p_ref, hp_ref, w_ref, b_ref, sc_ref, o_ref, e_ref, *, tt):
    i = pl.program_id(0)
    e_ref[POOL_HALO:POOL_HALO + tt, :] = p_ref[...].astype(F32)
    e_ref[0:POOL_HALO, :] = jnp.where(i > 0, hp_ref[...].astype(F32), 0.0)
    pos = (i * tt + 1 + lax.broadcasted_iota(jnp.int32, (tt, 1), 0)).astype(F32)
    for g, win in enumerate(POOL_WINDOWS):
        lanes = slice(g * POOL_GROUP, (g + 1) * POOL_GROUP)
        cur = e_ref[POOL_HALO:POOL_HALO + tt, lanes]
        tot = cur
        for lag in range(1, win):
            tot = tot + e_ref[POOL_HALO - lag:POOL_HALO - lag + tt, lanes]
        zg = tot / jnp.minimum(pos, float(win)) - cur
        out = jnp.dot(zg.astype(BF16), w_ref[g], preferred_element_type=F32) + b_ref[g:g + 1, :]
        o_ref[:, lanes] = (out * sc_ref[:, lanes]).astype(BF16)


def _pool(z, w_pool, b_pool, scale, *, tt=512):
    s = z.shape[0]
    w = BRANCH_WIDTH
    hb = tt // POOL_HALO
    return pl.pallas_call(
        functools.partial(_pool_kernel, tt=tt),
        out_shape=jax.ShapeDtypeStruct((s, w), BF16),
        grid=(s // tt,),
        in_specs=[
            pl.BlockSpec((tt, w), lambda i: (i, Z_POOL // w)),
            pl.BlockSpec((POOL_HALO, w), lambda i: (jnp.maximum(i * hb - 1, 0), Z_POOL // w)),
            pl.BlockSpec(w_pool.shape, lambda i: (0, 0, 0)),
            pl.BlockSpec(b_pool.shape, lambda i: (0, 0)),
            pl.BlockSpec((1, w), lambda i: (0, 0)),
        ],
        out_specs=pl.BlockSpec((tt, w), lambda i: (i, 0)),
        scratch_shapes=[pltpu.VMEM((POOL_HALO + tt, w), F32)],
        compiler_params=_params(("parallel",)),
        name="multiscale_pool",
    )(z, z, w_pool, b_pool, scale.reshape(1, w))


def _merge_kernel(u_ref, ya_ref, yb_ref, yc_ref, yd_ref, gw_ref, gb_ref, up_ref, o_ref):
    u = u_ref[...]
    acc = None
    for n, y_ref in enumerate((ya_ref, yb_ref, yc_ref, yd_ref)):
        gate = _sigmoid(jnp.dot(u, gw_ref[n], preferred_element_type=F32) + gb_ref[n:n + 1, :])
        term = gate * jnp.dot(y_ref[...], up_ref[n], preferred_element_type=F32)
        acc = term if acc is None else acc + term
    o_ref[...] = acc.astype(BF16)


def _merge(u, ys, gate_w, gate_b, up_w, *, tm=1024):
    s, d = u.shape
    nb, w = up_w.shape[0], up_w.shape[1]
    gb = GATE_BLOCK
    y_spec = pl.BlockSpec((tm, w), lambda i, k: (i, 0))
    return pl.pallas_call(
        _merge_kernel,
        out_shape=jax.ShapeDtypeStruct((s, d), BF16),
        grid=(s // tm, d // gb),
        in_specs=[
            pl.BlockSpec((tm, gb), lambda i, k: (i, k)),
            y_spec, y_spec, y_spec, y_spec,
            pl.BlockSpec((nb, None, gb, gb), lambda i, k: (0, k, 0, 0)),
            pl.BlockSpec((nb, gb), lambda i, k: (0, k)),
            pl.BlockSpec((nb, w, gb), lambda i, k: (0, 0, k)),
        ],
        out_specs=pl.BlockSpec((tm, gb), lambda i, k: (i, k)),
        compiler_params=_params(("parallel", "arbitrary")),
        name="gated_merge",
    )(u, *ys, gate_w, gate_b, up_w)


def _ffn_block(h, pre_g, w_gate, w_up, w_down, post_g):
    act = _ffn_up(h, pre_g, w_gate.astype(BF16), w_up.astype(BF16))
    return _proj_norm_res(act, w_down.astype(BF16), post_g, h, 0.5)


def _mixer_block(h, pre_g, w_in, forget_bias, ssm_params, ssm_d, ssm_w_glu, ssm_b_glu,
                 conv_w, conv_b, conv_ln_g, conv_ln_b, pool_w, pool_b, pool_scale,
                 branch_w_up, gate_w, gate_b, w_out, post_g):
    off_f = 3 * BRANCH_WIDTH
    w_q = w_in[:, :BRANCH_WIDTH] * (ATT_HEAD_DIM ** -0.5)
    w_main = jnp.concatenate([w_q, w_in[:, BRANCH_WIDTH:off_f], w_in[:, off_f + ATT_HEADS:]], axis=1).astype(BF16)
    w_f = jnp.zeros((HEAD_ROWS, D_MODEL), F32).at[:ATT_HEADS].set(w_in[:, off_f:off_f + ATT_HEADS].T).astype(BF16)
    b_f = jnp.zeros((HEAD_ROWS, 1), F32).at[:ATT_HEADS, 0].set(forget_bias)

    z, u = _in_proj(h, pre_g, w_main)
    c = _forget_cumsum(u, w_f, b_f)
    y_a = _attention(z, c)
    y_b = _ssm(z, _ssm_tables(*ssm_params), ssm_d, ssm_w_glu.astype(BF16), ssm_b_glu)
    y_c = _conv(z, conv_w, conv_b, conv_ln_g, conv_ln_b)
    y_d = _pool(z, pool_w.astype(BF16), pool_b, pool_scale)
    merged = _merge(u, (y_a, y_b, y_c, y_d), gate_w.astype(BF16), gate_b, branch_w_up.astype(BF16))
    return _proj_norm_res(merged, w_out.astype(BF16), post_g, h, 1.0)


def kernel(x, ffn1_pre_g, ffn1_w_gate, ffn1_w_up, ffn1_w_down, ffn1_post_g, mix_pre_g, w_in, forget_bias, ssm_lam_re, ssm_lam_im, ssm_log_dt, ssm_b_re, ssm_b_im, ssm_c_re, ssm_c_im, ssm_d, ssm_w_glu, ssm_b_glu, conv_w, conv_b, conv_ln_g, conv_ln_b, pool_w, pool_b, pool_scale, branch_w_up, gate_w, gate_b, w_out, mix_post_g, ffn2_pre_g, ffn2_w_gate, ffn2_w_up, ffn2_w_down, ffn2_post_g):
    batch, seq, d = x.shape
    outs = []
    for b in range(batch):
        h = x[b]
        for l in range(ffn1_pre_g.shape[0]):
            h = _ffn_block(h, ffn1_pre_g[l], ffn1_w_gate[l], ffn1_w_up[l], ffn1_w_down[l], ffn1_post_g[l])
            h = _mixer_block(
                h, mix_pre_g[l], w_in[l], forget_bias[l],
                (ssm_lam_re[l], ssm_lam_im[l], ssm_log_dt[l], ssm_b_re[l], ssm_b_im[l], ssm_c_re[l], ssm_c_im[l]),
                ssm_d[l], ssm_w_glu[l], ssm_b_glu[l], conv_w[l], conv_b[l], conv_ln_g[l], conv_ln_b[l],
                pool_w[l], pool_b[l], pool_scale[l], branch_w_up[l], gate_w[l], gate_b[l], w_out[l], mix_post_g[l])
            h = _ffn_block(h, ffn2_pre_g[l], ffn2_w_gate[l], ffn2_w_up[l], ffn2_w_down[l], ffn2_post_g[l])
        outs.append(h)
    return jnp.stack(outs)
```

```python
import functools
import math

import numpy as np
import jax
import jax.numpy as jnp
from jax import lax
from jax.experimental import pallas as pl
from jax.experimental.pallas import tpu as pltpu

F32 = jnp.float32
BF16 = jnp.bfloat16

D_MODEL = 4096
BRANCH_WIDTH = 1024
ATT_HEADS = 8
ATT_HEAD_DIM = 128
SSM_GROUP = 16
SSM_GROUPS = 64
SSM_STATE = 64
CONV_WIDTH = 31
POOL_WINDOWS = (2, 4, 8, 16)
POOL_GROUP = 256
GATE_BLOCKS = 16
GATE_BLOCK = 256
D_FF = 6144
EPS = 1e-6

V7X_VMEM_BYTES = 64 * 1024 * 1024
VMEM_LIMIT_BYTES = V7X_VMEM_BYTES - 8 * 1024 * 1024
SUBLANES = 8
LANES = 128

Z_Q, Z_K, Z_V, Z_SSM, Z_CVAL, Z_CGATE, Z_POOL = (n * BRANCH_WIDTH for n in range(7))
Z_WIDTH = 7 * BRANCH_WIDTH
HEAD_ROWS = 16

SSM_CHUNK = 256
SSM_SUB = SSM_CHUNK // SUBLANES
SSM_SLABS = 4
SSM_SLAB_IN = BRANCH_WIDTH // SSM_SLABS
SSM_STATES = SSM_GROUPS * SSM_STATE
SSM_SLAB_STATE = SSM_STATES // SSM_SLABS
SSM_STRIP = 512

CONV_HALO = 32
POOL_HALO = 16

NEG_BIG = -1e30
ATT_STABILISER_MARGIN = 60.0


def _params(semantics):
    return pltpu.CompilerParams(dimension_semantics=semantics, vmem_limit_bytes=VMEM_LIMIT_BYTES)


def _sigmoid(x):
    return 1.0 / (1.0 + jnp.exp(-x))


def _rms_scale(x):
    return lax.rsqrt(jnp.mean(x * x, axis=-1, keepdims=True) + EPS)


def _rms_norm_kernel(x_ref, g_ref, o_ref):
    x = x_ref[...]
    o_ref[...] = (x * _rms_scale(x) * g_ref[...]).astype(BF16)


def _rms_norm(x, g, *, tm=256):
    s, d = x.shape
    return pl.pallas_call(
        _rms_norm_kernel,
        out_shape=jax.ShapeDtypeStruct((s, d), BF16),
        grid=(s // tm,),
        in_specs=[pl.BlockSpec((tm, d), lambda i: (i, 0)), pl.BlockSpec((1, d), lambda i: (0, 0))],
        out_specs=pl.BlockSpec((tm, d), lambda i: (i, 0)),
        compiler_params=_params(("parallel",)),
        name="rms_norm",
    )(x, g.reshape(1, d))


def _ffn_up_kernel(x_ref, wg_ref, wu_ref, o_ref):
    x = x_ref[...]
    a = jnp.dot(x, wg_ref[...], preferred_element_type=F32)
    b = jnp.dot(x, wu_ref[...], preferred_element_type=F32)
    o_ref[...] = (a * _sigmoid(a) * b).astype(BF16)


def _ffn_up(xn, wg, wu, *, tm=1024, tn=512):
    s, d = xn.shape
    f = wg.shape[1]
    return pl.pallas_call(
        _ffn_up_kernel,
        out_shape=jax.ShapeDtypeStruct((s, f), BF16),
        grid=(s // tm, f // tn),
        in_specs=[
            pl.BlockSpec((tm, d), lambda i, j: (i, 0)),
            pl.BlockSpec((d, tn), lambda i, j: (0, j)),
            pl.BlockSpec((d, tn), lambda i, j: (0, j)),
        ],
        out_specs=pl.BlockSpec((tm, tn), lambda i, j: (i, j)),
        compiler_params=_params(("parallel", "arbitrary")),
        name="ffn_up",
    )(xn, wg, wu)


def _proj_norm_res_kernel(a_ref, w_ref, g_ref, gn_ref, h_ref, o_ref, xn_ref, *, alpha, col_chunk, row_chunk):
    k = pl.program_id(1)
    tm, d = o_ref.shape

    @pl.when(k == 0)
    def _():
        o_ref[...] = jnp.zeros_like(o_ref)

    a = a_ref[...]
    for c0 in range(0, d, col_chunk):
        cols = slice(c0, c0 + col_chunk)
        o_ref[:, cols] += jnp.dot(a, w_ref[:, cols], preferred_element_type=F32)

    @pl.when(k == pl.num_programs(1) - 1)
    def _():
        for r0 in range(0, tm, row_chunk):
            rows = slice(r0, r0 + row_chunk)
            acc = o_ref[rows, :]
            h_new = h_ref[rows, :] + alpha * (acc * _rms_scale(acc) * g_ref[...])
            o_ref[rows, :] = h_new
            if xn_ref is not None:
                xn_ref[rows, :] = (h_new * _rms_scale(h_new) * gn_ref[...]).astype(BF16)


def _proj_norm_res_last_kernel(a_ref, w_ref, g_ref, h_ref, o_ref, **kw):
    _proj_norm_res_kernel(a_ref, w_ref, g_ref, None, h_ref, o_ref, None, **kw)


def _proj_norm_res(a, w, g, h, alpha, g_next, *, tm=512, tk=512):
    s, kdim = a.shape
    d = w.shape[1]
    row_spec = pl.BlockSpec((tm, d), lambda i, k: (i, 0))
    gain_spec = pl.BlockSpec((1, d), lambda i, k: (0, 0))
    statics = dict(alpha=alpha, col_chunk=1024, row_chunk=64)
    common = dict(grid=(s // tm, kdim // tk), compiler_params=_params(("parallel", "arbitrary")))
    mat_specs = [pl.BlockSpec((tm, tk), lambda i, k: (i, k)), pl.BlockSpec((tk, d), lambda i, k: (k, 0))]
    if g_next is None:
        h_new = pl.pallas_call(
            functools.partial(_proj_norm_res_last_kernel, **statics),
            out_shape=jax.ShapeDtypeStruct((s, d), F32),
            in_specs=mat_specs + [gain_spec, row_spec], out_specs=row_spec,
            name="proj_norm_res_last", **common,
        )(a, w, g.reshape(1, d), h)
        return h_new, None
    return pl.pallas_call(
        functools.partial(_proj_norm_res_kernel, **statics),
        out_shape=(jax.ShapeDtypeStruct((s, d), F32), jax.ShapeDtypeStruct((s, d), BF16)),
        in_specs=mat_specs + [gain_spec, gain_spec, row_spec], out_specs=(row_spec, row_spec),
        name="proj_norm_res", **common,
    )(a, w, g.reshape(1, d), g_next.reshape(1, d), h)


def _in_proj_kernel(u_ref, w_ref, z_ref):
    z_ref[...] = jnp.dot(u_ref[...], w_ref[...], preferred_element_type=F32).astype(BF16)


def _in_proj(u, w, *, tm=1024, tn=1024):
    s, d = u.shape
    n = w.shape[1]
    return pl.pallas_call(
        _in_proj_kernel,
        out_shape=jax.ShapeDtypeStruct((s, n), BF16),
        grid=(s // tm, n // tn),
        in_specs=[
            pl.BlockSpec((tm, d), lambda i, j: (i, 0)),
            pl.BlockSpec((d, tn), lambda i, j: (0, j)),
        ],
        out_specs=pl.BlockSpec((tm, tn), lambda i, j: (i, j)),
        compiler_params=_params(("parallel", "arbitrary")),
        name="in_proj",
    )(u, w)


def _split_bf16(x, parts=3):
    out = []
    for _ in range(parts):
        p = x.astype(BF16)
        out.append(p)
        x = x - p.astype(F32)
    return out


def _forget_kernel(u_ref, wf_ref, bf_ref, tri_ref, c_ref, carry_ref):
    @pl.when(pl.program_id(0) == 0)
    def _():
        carry_ref[...] = jnp.zeros_like(carry_ref)

    f = lax.dot_general(wf_ref[...], u_ref[...], (((1,), (1,)), ((), ())),
                        preferred_element_type=F32)
    x = f + bf_ref[...]
    log_f = jnp.minimum(x, 0.0) - jnp.log(1.0 + jnp.exp(-jnp.abs(x)))
    tri = tri_ref[...]
    cs = sum(jnp.dot(p, tri, preferred_element_type=F32) for p in _split_bf16(log_f))
    c = cs + carry_ref[...]
    c_ref[...] = c
    carry_ref[...] = c[:, c.shape[1] - 1:]


def _forget_cumsum(u, wf_t, bf, *, tm=512):
    s, d = u.shape
    tri = jnp.asarray(np.triu(np.ones((tm, tm), np.float32)), BF16)
    return pl.pallas_call(
        _forget_kernel,
        out_shape=jax.ShapeDtypeStruct((HEAD_ROWS, s), F32),
        grid=(s // tm,),
        in_specs=[
            pl.BlockSpec((tm, d), lambda i: (i, 0)),
            pl.BlockSpec((HEAD_ROWS, d), lambda i: (0, 0)),
            pl.BlockSpec((HEAD_ROWS, 1), lambda i: (0, 0)),
            pl.BlockSpec((tm, tm), lambda i: (0, 0)),
        ],
        out_specs=pl.BlockSpec((HEAD_ROWS, tm), lambda i: (0, i)),
        scratch_shapes=[pltpu.VMEM((HEAD_ROWS, 1), F32)],
        compiler_params=_params(("arbitrary",)),
        name="forget_cumsum",
    )(u, wf_t, bf, tri)


def _lane_groups(x):
    return [x[:, g * LANES:(g + 1) * LANES] for g in range(x.shape[1] // LANES)]


def _max_row_sq_norm(x):
    xf = x.astype(F32)
    return jnp.max(jnp.sum(xf * xf, axis=1, keepdims=True), axis=0, keepdims=True)


def _attn_kernel(q_ref, k_ref, v_ref, c_ref, o_ref, vaug_ref, kn_ref, m_ref, acc_ref, *, tq):
    h = pl.program_id(0)
    i = pl.program_id(1)
    dh = ATT_HEAD_DIM
    seq = k_ref.shape[0]

    @pl.when(i == 0)
    def _():
        vaug_ref[:, 0:dh] = v_ref[...]
        vaug_ref[:, dh:2 * dh] = jnp.ones((seq, dh), BF16)

        def kn_body(t, mx):
            return jnp.maximum(mx, _max_row_sq_norm(k_ref[pl.ds(pl.multiple_of(t * tq, tq), tq), :]))

        kn2 = lax.fori_loop(0, seq // tq, kn_body, jnp.zeros((1, 1), F32))
        kn_ref[...] = jnp.broadcast_to(kn2, kn_ref.shape)

    q = q_ref[...]
    start = pl.multiple_of(i * tq, tq)
    c_q = c_ref[pl.ds(h, 1), pl.ds(start, tq)]
    c_base = c_q[:, 0:1]
    nt_dims = (((1,), (1,)), ((), ()))

    def scores(off, width):
        c_k = c_ref[pl.ds(h, 1), pl.ds(off, width)] - c_base
        return lax.dot_general(q, k_ref[pl.ds(off, width), :], nt_dims, preferred_element_type=F32) - c_k

    def diag_scores():
        rows = lax.broadcasted_iota(jnp.int32, (tq, tq), 0)
        cols = lax.broadcasted_iota(jnp.int32, (tq, tq), 1)
        return jnp.where(cols <= rows, scores(start, tq), NEG_BIG)

    def lane_max(s):
        return functools.reduce(jnp.maximum, _lane_groups(s))

    def probs(s):
        m_rep = m_ref[...]
        return jnp.concatenate([jnp.exp(g - m_rep) for g in _lane_groups(s)], axis=1).astype(BF16)

    def update_max(off, width):
        m_ref[...] = jnp.maximum(m_ref[...], lane_max(scores(off, width)))

    def accumulate(off, width):
        acc_ref[...] += jnp.dot(probs(scores(off, width)), vaug_ref[pl.ds(off, width), :],
                                preferred_element_type=F32)

    bound = jnp.sqrt(_max_row_sq_norm(q) * kn_ref[0:1, 0:1]) * 1.001
    threshold = 2.0 * bound - ATT_STABILISER_MARGIN
    pos = lax.broadcasted_iota(jnp.int32, (1, seq), 1)
    block_end = (pos & (tq - 1)) == (tq - 1)
    near = jnp.where(block_end, c_ref[pl.ds(h, 1), :] - c_base, threshold) < threshold
    n_near = jnp.sum(jnp.where(near & (pos < start), 1.0, 0.0)).astype(jnp.int32)

    m_ref[...] = lane_max(diag_scores())

    def near_body(t, carry):
        update_max(pl.multiple_of((i - 1 - t) * tq, tq), tq)
        return carry

    lax.fori_loop(0, n_near, near_body, 0)
    m_ref[...] = jnp.broadcast_to(jnp.max(m_ref[...], axis=1, keepdims=True), m_ref.shape)

    acc_ref[...] = jnp.dot(probs(diag_scores()), vaug_ref[pl.ds(start, tq), :], preferred_element_type=F32)

    def wide_body(jj, carry):
        accumulate(pl.multiple_of(jj * (4 * tq), 4 * tq), 4 * tq)
        return carry

    lax.fori_loop(0, i // 4, wide_body, 0)

    @pl.when((i & 2) != 0)
    def _():
        accumulate(pl.multiple_of((i // 4) * (4 * tq), 2 * tq), 2 * tq)

    @pl.when((i & 1) != 0)
    def _():
        accumulate(pl.multiple_of((i - 1) * tq, tq), tq)

    o_ref[...] = (acc_ref[:, 0:dh] / acc_ref[:, dh:dh + 1]).astype(BF16)


def _attention(z, c, *, tq=512):
    s = z.shape[0]
    dh = ATT_HEAD_DIM
    return pl.pallas_call(
        functools.partial(_attn_kernel, tq=tq),
        out_shape=jax.ShapeDtypeStruct((s, BRANCH_WIDTH), BF16),
        grid=(ATT_HEADS, s // tq),
        in_specs=[
            pl.BlockSpec((tq, dh), lambda h, i: (i, Z_Q // dh + h)),
            pl.BlockSpec((s, dh), lambda h, i: (0, Z_K // dh + h)),
            pl.BlockSpec((s, dh), lambda h, i: (0, Z_V // dh + h)),
            pl.BlockSpec((HEAD_ROWS, s), lambda h, i: (0, 0)),
        ],
        out_specs=pl.BlockSpec((tq, dh), lambda h, i: (i, h)),
        scratch_shapes=[pltpu.VMEM((s, 2 * dh), BF16), pltpu.VMEM((SUBLANES, LANES), F32),
                        pltpu.VMEM((tq, LANES), F32),
                        pltpu.VMEM((tq, 2 * dh), F32)],
        compiler_params=_params(("parallel", "arbitrary")),
        name="forget_attention",
    )(z, z, z, c)


def _cmul(ar, ai, br, bi):
    return ar * br - ai * bi, ar * bi + ai * br


def _gelu_tanh(x):
    return 0.5 * x * (1.0 + jnp.tanh(math.sqrt(2.0 / math.pi) * (x + 0.044715 * (x * x * x))))


def _ssm_kernel(u_ref, perm_ref, perm_t_ref, b_re_ref, b_im_ref, c_re_ref, c_im_ref,
                lam_ref, lam_sub_ref, pow_sub_ref, pow_step_ref, d_ref, wglu_ref, bglu_ref,
                o_ref, x_re_ref, x_im_ref, cin_ref, carry_ref):
    @pl.when(pl.program_id(0) == 0)
    def _():
        carry_ref[...] = jnp.zeros_like(carry_ref)

    n_sub = SSM_SUB
    n_strips = SSM_STATES // SSM_STRIP
    u_p = jnp.dot(perm_ref[...], u_ref[...], preferred_element_type=F32).astype(BF16)

    for sl in range(SSM_SLABS):
        u_sl = u_p[:, sl * SSM_SLAB_IN:(sl + 1) * SSM_SLAB_IN]
        cols = slice(sl * SSM_SLAB_STATE, (sl + 1) * SSM_SLAB_STATE)
        x_re_ref[:, cols] = jnp.dot(u_sl, b_re_ref[sl], preferred_element_type=F32)
        x_im_ref[:, cols] = jnp.dot(u_sl, b_im_ref[sl], preferred_element_type=F32)

    def scan_strip(st, carry):
        lanes = pl.ds(pl.multiple_of(st * SSM_STRIP, SSM_STRIP), SSM_STRIP)
        lr = jnp.broadcast_to(lam_ref[0:1, lanes], (SUBLANES, SSM_STRIP))
        li = jnp.broadcast_to(lam_ref[1:2, lanes], (SUBLANES, SSM_STRIP))
        sr = x_re_ref[0:SUBLANES, lanes]
        si = x_im_ref[0:SUBLANES, lanes]
        for i in range(1, n_sub):
            rows = slice(i * SUBLANES, (i + 1) * SUBLANES)
            pr, pi = _cmul(lr, li, sr, si)
            sr = pr + x_re_ref[rows, lanes]
            si = pi + x_im_ref[rows, lanes]
            x_re_ref[rows, lanes] = sr
            x_im_ref[rows, lanes] = si
        return carry

    lax.fori_loop(0, n_strips, scan_strip, 0)

    last = slice((n_sub - 1) * SUBLANES, n_sub * SUBLANES)
    tr = x_re_ref[last, :]
    ti = x_im_ref[last, :]
    row = lax.broadcasted_iota(jnp.int32, (SUBLANES, SSM_STATES), 0)
    for lvl, dist in enumerate((1, 2, 4)):
        ar = lam_sub_ref[2 * lvl:2 * lvl + 1, :]
        ai = lam_sub_ref[2 * lvl + 1:2 * lvl + 2, :]
        shr = jnp.where(row >= dist, pltpu.roll(tr, dist, 0), 0.0)
        shi = jnp.where(row >= dist, pltpu.roll(ti, dist, 0), 0.0)
        pr, pi = _cmul(ar, ai, shr, shi)
        tr = tr + pr
        ti = ti + pi
    in_r = carry_ref[0:1, :]
    in_i = carry_ref[1:2, :]
    pr, pi = _cmul(pow_sub_ref[0:SUBLANES, :], pow_sub_ref[SUBLANES:2 * SUBLANES, :], in_r, in_i)
    tr = tr + pr
    ti = ti + pi
    cin_ref[0:SUBLANES, :] = jnp.where(row >= 1, pltpu.roll(tr, 1, 0), in_r)
    cin_ref[SUBLANES:2 * SUBLANES, :] = jnp.where(row >= 1, pltpu.roll(ti, 1, 0), in_i)
    carry_ref[0:1, :] = tr[SUBLANES - 1:SUBLANES, :]
    carry_ref[1:2, :] = ti[SUBLANES - 1:SUBLANES, :]

    def fix_strip(st, carry):
        lanes = pl.ds(pl.multiple_of(st * SSM_STRIP, SSM_STRIP), SSM_STRIP)
        cr = cin_ref[0:SUBLANES, lanes]
        ci = cin_ref[SUBLANES:2 * SUBLANES, lanes]
        for i in range(n_sub):
            rows = slice(i * SUBLANES, (i + 1) * SUBLANES)
            pr, pi = _cmul(pow_step_ref[i:i + 1, lanes], pow_step_ref[n_sub + i:n_sub + i + 1, lanes], cr, ci)
            x_re_ref[rows, lanes] = x_re_ref[rows, lanes] + pr
            x_im_ref[rows, lanes] = x_im_ref[rows, lanes] + pi
        return carry

    lax.fori_loop(0, n_strips, fix_strip, 0)

    ys = []
    for sl in range(SSM_SLABS):
        cols = slice(sl * SSM_SLAB_STATE, (sl + 1) * SSM_SLAB_STATE)
        ys.append(jnp.dot(x_re_ref[:, cols].astype(BF16), c_re_ref[sl], preferred_element_type=F32)
                  - jnp.dot(x_im_ref[:, cols].astype(BF16), c_im_ref[sl], preferred_element_type=F32))
    y = jnp.concatenate(ys, axis=1) + d_ref[...] * u_p.astype(F32)
    y = _gelu_tanh(y)
    gate = jnp.dot(y.astype(BF16), wglu_ref[...], preferred_element_type=F32) + bglu_ref[...]
    y = (y * _sigmoid(gate)).astype(BF16)
    o_ref[...] = jnp.dot(perm_t_ref[...], y, preferred_element_type=F32).astype(BF16)


def _ssm_tables(lam_re, lam_im, log_dt, b_re, b_im, c_re, c_im):
    dt = jnp.exp(log_dt)[:, None]
    a_re = lam_re * dt
    a_im = lam_im * dt

    def lam_pow(n):
        mag = jnp.exp(a_re * n)
        return (mag * jnp.cos(a_im * n)).reshape(-1), (mag * jnp.sin(a_im * n)).reshape(-1)

    lb_re = jnp.exp(a_re) * jnp.cos(a_im)
    lb_im = jnp.exp(a_re) * jnp.sin(a_im)
    den = lam_re * lam_re + lam_im * lam_im
    k_re = ((lb_re - 1.0) * lam_re + lb_im * lam_im) / den
    k_im = (lb_im * lam_re - (lb_re - 1.0) * lam_im) / den
    bb_re = k_re[..., None] * b_re - k_im[..., None] * b_im
    bb_im = k_re[..., None] * b_im + k_im[..., None] * b_re

    g, p, hg = b_re.shape
    gs = g // SSM_SLABS
    eye = jnp.eye(gs, dtype=F32)

    def b_slabs(x):
        x = x.reshape(SSM_SLABS, gs, p, hg)
        return jnp.einsum('sgph,gk->sghkp', x, eye).reshape(SSM_SLABS, gs * hg, gs * p).astype(BF16)

    def c_slabs(x):
        x = x.reshape(SSM_SLABS, gs, hg, p)
        return jnp.einsum('sghp,gk->sgpkh', x, eye).reshape(SSM_SLABS, gs * p, gs * hg).astype(BF16)

    lam = jnp.stack(lam_pow(1.0))
    lam_sub = jnp.concatenate([jnp.stack(lam_pow(float(SSM_SUB * d))) for d in (1, 2, 4)])
    sub_pows = [lam_pow(float(SSM_SUB * (j + 1))) for j in range(SUBLANES)]
    pow_sub = jnp.stack([x[0] for x in sub_pows] + [x[1] for x in sub_pows])
    step_pows = [lam_pow(float(i + 1)) for i in range(SSM_SUB)]
    pow_step = jnp.stack([x[0] for x in step_pows] + [x[1] for x in step_pows])
    return b_slabs(bb_re), b_slabs(bb_im), c_slabs(c_re), c_slabs(c_im), lam, lam_sub, pow_sub, pow_step


def _ssm_permutation():
    perm = np.zeros((SSM_CHUNK, SSM_CHUNK), np.float32)
    for j in range(SUBLANES):
        for i in range(SSM_SUB):
            perm[i * SUBLANES + j, j * SSM_SUB + i] = 1.0
    return jnp.asarray(perm, BF16), jnp.asarray(perm.T, BF16)


def _ssm(z, tables, d_skip, w_glu, b_glu):
    s = z.shape[0]
    w = BRANCH_WIDTH
    perm, perm_t = _ssm_permutation()
    b_re, b_im, c_re, c_im, lam, lam_sub, pow_sub, pow_step = tables
    full = lambda a: pl.BlockSpec(a.shape, lambda c, _n=a.ndim: (0,) * _n)
    operands = (perm, perm_t, b_re, b_im, c_re, c_im, lam, lam_sub, pow_sub, pow_step,
                d_skip.reshape(1, w), w_glu, b_glu.reshape(1, w))
    return pl.pallas_call(
        _ssm_kernel,
        out_shape=jax.ShapeDtypeStruct((s, w), BF16),
        grid=(s // SSM_CHUNK,),
        in_specs=[pl.BlockSpec((SSM_CHUNK, w), lambda c: (c, Z_SSM // w))] + [full(a) for a in operands],
        out_specs=pl.BlockSpec((SSM_CHUNK, w), lambda c: (c, 0)),
        scratch_shapes=[
            pltpu.VMEM((SSM_CHUNK, SSM_STATES), F32),
            pltpu.VMEM((SSM_CHUNK, SSM_STATES), F32),
            pltpu.VMEM((2 * SUBLANES, SSM_STATES), F32),
            pltpu.VMEM((2, SSM_STATES), F32),
        ],
        compiler_params=_params(("arbitrary",)),
        name="s5_ssm",
    )(z, *operands)


def _conv_kernel(val_ref, gate_ref, hval_ref, hgate_ref, w_ref, b_ref, lng_ref, lnb_ref, o_ref,
                 a_ref, y_ref, *, tt, row_chunk, lane_chunk):
    a_ref[CONV_HALO:CONV_HALO + tt, :] = val_ref[...].astype(F32) * _sigmoid(gate_ref[...].astype(F32))
    halo = hval_ref[...].astype(F32) * _sigmoid(hgate_ref[...].astype(F32))
    a_ref[0:CONV_HALO, :] = jnp.where(pl.program_id(0) > 0, halo, 0.0)

    first = CONV_HALO - (CONV_WIDTH - 1)
    for c0 in range(0, BRANCH_WIDTH, lane_chunk):
        lanes = slice(c0, c0 + lane_chunk)
        for r0 in range(0, tt, row_chunk):
            acc = jnp.broadcast_to(b_ref[:, lanes], (row_chunk, lane_chunk))
            for j in range(CONV_WIDTH):
                acc = acc + w_ref[j:j + 1, lanes] * a_ref[r0 + first + j:r0 + first + j + row_chunk, lanes]
            y_ref[r0:r0 + row_chunk, lanes] = acc

    y = y_ref[...]
    mu = jnp.mean(y, axis=-1, keepdims=True)
    yc = y - mu
    var = jnp.mean(yc * yc, axis=-1, keepdims=True)
    yn = yc * lax.rsqrt(var + EPS) * lng_ref[...] + lnb_ref[...]
    o_ref[...] = (yn * _sigmoid(yn)).astype(BF16)


def _conv(z, w_dw, b_dw, ln_g, ln_b, *, tt=256):
    s = z.shape[0]
    w = BRANCH_WIDTH
    hb = tt // CONV_HALO
    halo_map = lambda col: (lambda i: (jnp.maximum(i * hb - 1, 0), col))
    return pl.pallas_call(
        functools.partial(_conv_kernel, tt=tt, row_chunk=64, lane_chunk=256),
        out_shape=jax.ShapeDtypeStruct((s, w), BF16),
        grid=(s // tt,),
        in_specs=[
            pl.BlockSpec((tt, w), lambda i: (i, Z_CVAL // w)),
            pl.BlockSpec((tt, w), lambda i: (i, Z_CGATE // w)),
            pl.BlockSpec((CONV_HALO, w), halo_map(Z_CVAL // w)),
            pl.BlockSpec((CONV_HALO, w), halo_map(Z_CGATE // w)),
            pl.BlockSpec((CONV_WIDTH, w), lambda i: (0, 0)),
            pl.BlockSpec((1, w), lambda i: (0, 0)),
            pl.BlockSpec((1, w), lambda i: (0, 0)),
            pl.BlockSpec((1, w), lambda i: (0, 0)),
        ],
        out_specs=pl.BlockSpec((tt, w), lambda i: (i, 0)),
        scratch_shapes=[pltpu.VMEM((CONV_HALO + tt, w), F32), pltpu.VMEM((tt, w), F32)],
        compiler_params=_params(("parallel",)),
        name="conformer_conv",
    )(z, z, z, z, w_dw, b_dw.reshape(1, w), ln_g.reshape(1, w), ln_b.reshape(1, w))


def _pool_kernel(p_ref, hp_ref, w_ref, b_ref, sc_ref, o_ref, e_ref, *, tt):
    i = pl.program_id(0)
    e_ref[POOL_HALO:POOL_HALO + tt, :] = p_ref[...].astype(F32)
    e_ref[0:POOL_HALO, :] = jnp.where(i > 0, hp_ref[...].astype(F32), 0.0)
    pos = (i * tt + 1 + lax.broadcasted_iota(jnp.int32, (tt, 1), 0)).astype(F32)
    for g, win in enumerate(POOL_WINDOWS):
        lanes = slice(g * POOL_GROUP, (g + 1) * POOL_GROUP)
        cur = e_ref[POOL_HALO:POOL_HALO + tt, lanes]
        tot = cur
        for lag in range(1, win):
            tot = tot + e_ref[POOL_HALO - lag:POOL_HALO - lag + tt, lanes]
        zg = tot / jnp.minimum(pos, float(win)) - cur
        out = jnp.dot(zg.astype(BF16), w_ref[g], preferred_element_type=F32) + b_ref[g:g + 1, :]
        o_ref[:, lanes] = (out * sc_ref[:, lanes]).astype(BF16)


def _pool(z, w_pool, b_pool, scale, *, tt=512):
    s = z.shape[0]
    w = BRANCH_WIDTH
    hb = tt // POOL_HALO
    return pl.pallas_call(
        functools.partial(_pool_kernel, tt=tt),
        out_shape=jax.ShapeDtypeStruct((s, w), BF16),
        grid=(s // tt,),
        in_specs=[
            pl.BlockSpec((tt, w), lambda i: (i, Z_POOL // w)),
            pl.BlockSpec((POOL_HALO, w), lambda i: (jnp.maximum(i * hb - 1, 0), Z_POOL // w)),
            pl.BlockSpec(w_pool.shape, lambda i: (0, 0, 0)),
            pl.BlockSpec(b_pool.shape, lambda i: (0, 0)),
            pl.BlockSpec((1, w), lambda i: (0, 0)),
        ],
        out_specs=pl.BlockSpec((tt, w), lambda i: (i, 0)),
        scratch_shapes=[pltpu.VMEM((POOL_HALO + tt, w), F32)],
        compiler_params=_params(("parallel",)),
        name="multiscale_pool",
    )(z, z, w_pool, b_pool, scale.reshape(1, w))


def _merge_kernel(u_ref, ya_ref, yb_ref, yc_ref, yd_ref, gw_ref, gb_ref, up_ref, o_ref):
    u = u_ref[...]
    acc = None
    for n, y_ref in enumerate((ya_ref, yb_ref, yc_ref, yd_ref)):
        gate = _sigmoid(jnp.dot(u, gw_ref[n], preferred_element_type=F32) + gb_ref[n:n + 1, :])
        term = gate * jnp.dot(y_ref[...], up_ref[n], preferred_element_type=F32)
        acc = term if acc is None else acc + term
    o_ref[...] = acc.astype(BF16)


def _merge(u, ys, gate_w, gate_b, up_w, *, tm=1024):
    s, d = u.shape
    nb, w = up_w.shape[0], up_w.shape[1]
    gb = GATE_BLOCK
    y_spec = pl.BlockSpec((tm, w), lambda i, k: (i, 0))
    return pl.pallas_call(
        _merge_kernel,
        out_shape=jax.ShapeDtypeStruct((s, d), BF16),
        grid=(s // tm, d // gb),
        in_specs=[
            pl.BlockSpec((tm, gb), lambda i, k: (i, k)),
            y_spec, y_spec, y_spec, y_spec,
            pl.BlockSpec((nb, None, gb, gb), lambda i, k: (0, k, 0, 0)),
            pl.BlockSpec((nb, gb), lambda i, k: (0, k)),
            pl.BlockSpec((nb, w, gb), lambda i, k: (0, 0, k)),
        ],
        out_specs=pl.BlockSpec((tm, gb), lambda i, k: (i, k)),
        compiler_params=_params(("parallel", "arbitrary")),
        name="gated_merge",
    )(u, *ys, gate_w, gate_b, up_w)


def _ffn_block(h, xn, w_gate, w_up, w_down, post_g, g_next):
    act = _ffn_up(xn, w_gate.astype(BF16), w_up.astype(BF16))
    return _proj_norm_res(act, w_down.astype(BF16), post_g, h, 0.5, g_next)


def _mixer_block(h, u, w_in, forget_bias, ssm_params, ssm_d, ssm_w_glu, ssm_b_glu,
                 conv_w, conv_b, conv_ln_g, conv_ln_b, pool_w, pool_b, pool_scale,
                 branch_w_up, gate_w, gate_b, w_out, post_g, g_next):
    off_f = 3 * BRANCH_WIDTH
    w_q = w_in[:, :BRANCH_WIDTH] * (ATT_HEAD_DIM ** -0.5)
    w_main = jnp.concatenate([w_q, w_in[:, BRANCH_WIDTH:off_f], w_in[:, off_f + ATT_HEADS:]], axis=1).astype(BF16)
    w_f = jnp.zeros((HEAD_ROWS, D_MODEL), F32).at[:ATT_HEADS].set(w_in[:, off_f:off_f + ATT_HEADS].T).astype(BF16)
    b_f = jnp.zeros((HEAD_ROWS, 1), F32).at[:ATT_HEADS, 0].set(forget_bias)

    z = _in_proj(u, w_main)
    c = _forget_cumsum(u, w_f, b_f)
    y_a = _attention(z, c)
    y_b = _ssm(z, _ssm_tables(*ssm_params), ssm_d, ssm_w_glu.astype(BF16), ssm_b_glu)
    y_c = _conv(z, conv_w, conv_b, conv_ln_g, conv_ln_b)
    y_d = _pool(z, pool_w.astype(BF16), pool_b, pool_scale)
    merged = _merge(u, (y_a, y_b, y_c, y_d), gate_w.astype(BF16), gate_b, branch_w_up.astype(BF16))
    return _proj_norm_res(merged, w_out.astype(BF16), post_g, h, 1.0, g_next)


def kernel(x, ffn1_pre_g, ffn1_w_gate, ffn1_w_up, ffn1_w_down, ffn1_post_g, mix_pre_g, w_in, forget_bias, ssm_lam_re, ssm_lam_im, ssm_log_dt, ssm_b_re, ssm_b_im, ssm_c_re, ssm_c_im, ssm_d, ssm_w_glu, ssm_b_glu, conv_w, conv_b, conv_ln_g, conv_ln_b, pool_w, pool_b, pool_scale, branch_w_up, gate_w, gate_b, w_out, mix_post_g, ffn2_pre_g, ffn2_w_gate, ffn2_w_up, ffn2_w_down, ffn2_post_g):
    batch, seq, d = x.shape
    outs = []
    for b in range(batch):
        h = x[b]
        depth = ffn1_pre_g.shape[0]
        xn = _rms_norm(h, ffn1_pre_g[0])
        for l in range(depth):
            h, xn = _ffn_block(h, xn, ffn1_w_gate[l], ffn1_w_up[l], ffn1_w_down[l], ffn1_post_g[l], mix_pre_g[l])
            h, xn = _mixer_block(
                h, xn, w_in[l], forget_bias[l],
                (ssm_lam_re[l], ssm_lam_im[l], ssm_log_dt[l], ssm_b_re[l], ssm_b_im[l], ssm_c_re[l], ssm_c_im[l]),
                ssm_d[l], ssm_w_glu[l], ssm_b_glu[l], conv_w[l], conv_b[l], conv_ln_g[l], conv_ln_b[l],
                pool_w[l], pool_b[l], pool_scale[l], branch_w_up[l], gate_w[l], gate_b[l], w_out[l], mix_post_g[l],
                ffn2_pre_g[l])
            g_next = ffn1_pre_g[l + 1] if l + 1 < depth else None
            h, xn = _ffn_block(h, xn, ffn2_w_gate[l], ffn2_w_up[l], ffn2_w_down[l], ffn2_post_g[l], g_next)
        outs.append(h)
    return jnp.stack(outs)
```

```python
import functools
import math

import numpy as np
import jax
import jax.numpy as jnp
from jax import lax
from jax.experimental import pallas as pl
from jax.experimental.pallas import tpu as pltpu

F32 = jnp.float32
BF16 = jnp.bfloat16

D_MODEL = 4096
BRANCH_WIDTH = 1024
ATT_HEADS = 8
ATT_HEAD_DIM = 128
SSM_GROUP = 16
SSM_GROUPS = 64
SSM_STATE = 64
CONV_WIDTH = 31
POOL_WINDOWS = (2, 4, 8, 16)
POOL_GROUP = 256
GATE_BLOCKS = 16
GATE_BLOCK = 256
D_FF = 6144
EPS = 1e-6

V7X_VMEM_BYTES = 64 * 1024 * 1024
VMEM_LIMIT_BYTES = V7X_VMEM_BYTES - 8 * 1024 * 1024
SUBLANES = 8
LANES = 128

Z_Q, Z_K, Z_V, Z_SSM, Z_CVAL, Z_CGATE, Z_POOL = (n * BRANCH_WIDTH for n in range(7))
Z_WIDTH = 7 * BRANCH_WIDTH
HEAD_ROWS = 16

SSM_CHUNK = 256
SSM_SUB = SSM_CHUNK // SUBLANES
SSM_SLABS = 4
SSM_SLAB_IN = BRANCH_WIDTH // SSM_SLABS
SSM_STATES = SSM_GROUPS * SSM_STATE
SSM_SLAB_STATE = SSM_STATES // SSM_SLABS
SSM_STRIP = 512

CONV_HALO = 32
POOL_HALO = 16

NEG_BIG = -1e30
ATT_STABILISER_MARGIN = 60.0


def _params(semantics):
    return pltpu.CompilerParams(dimension_semantics=semantics, vmem_limit_bytes=VMEM_LIMIT_BYTES)


def _sigmoid(x):
    return 1.0 / (1.0 + jnp.exp(-x))


def _rms_scale(x):
    return lax.rsqrt(jnp.mean(x * x, axis=-1, keepdims=True) + EPS)


def _rms_norm_kernel(x_ref, g_ref, o_ref):
    x = x_ref[...]
    o_ref[...] = (x * _rms_scale(x) * g_ref[...]).astype(BF16)


def _rms_norm(x, g, *, tm=256):
    s, d = x.shape
    return pl.pallas_call(
        _rms_norm_kernel,
        out_shape=jax.ShapeDtypeStruct((s, d), BF16),
        grid=(s // tm,),
        in_specs=[pl.BlockSpec((tm, d), lambda i: (i, 0)), pl.BlockSpec((1, d), lambda i: (0, 0))],
        out_specs=pl.BlockSpec((tm, d), lambda i: (i, 0)),
        compiler_params=_params(("parallel",)),
        name="rms_norm",
    )(x, g.reshape(1, d))


def _ffn_up_kernel(x_ref, wg_ref, wu_ref, o_ref):
    x = x_ref[...]
    a = jnp.dot(x, wg_ref[...], preferred_element_type=F32)
    b = jnp.dot(x, wu_ref[...], preferred_element_type=F32)
    o_ref[...] = (a * _sigmoid(a) * b).astype(BF16)


def _ffn_up(xn, wg, wu, *, tm=1024, tn=512):
    s, d = xn.shape
    f = wg.shape[1]
    return pl.pallas_call(
        _ffn_up_kernel,
        out_shape=jax.ShapeDtypeStruct((s, f), BF16),
        grid=(s // tm, f // tn),
        in_specs=[
            pl.BlockSpec((tm, d), lambda i, j: (i, 0)),
            pl.BlockSpec((d, tn), lambda i, j: (0, j)),
            pl.BlockSpec((d, tn), lambda i, j: (0, j)),
        ],
        out_specs=pl.BlockSpec((tm, tn), lambda i, j: (i, j)),
        compiler_params=_params(("parallel", "arbitrary")),
        name="ffn_up",
    )(xn, wg, wu)


def _proj_norm_res_kernel(a_ref, w_ref, g_ref, gn_ref, h_ref, o_ref, xn_ref, *, alpha, col_chunk, row_chunk):
    k = pl.program_id(1)
    tm, d = o_ref.shape

    def partial_products(accumulate):
        a = a_ref[...]
        for c0 in range(0, d, col_chunk):
            cols = slice(c0, c0 + col_chunk)
            t = jnp.dot(a, w_ref[:, cols], preferred_element_type=F32)
            if accumulate:
                o_ref[:, cols] += t
            else:
                o_ref[:, cols] = t

    @pl.when(k == 0)
    def _():
        partial_products(False)

    @pl.when(k > 0)
    def _():
        partial_products(True)

    @pl.when(k == pl.num_programs(1) - 1)
    def _():
        ga = alpha * g_ref[...]
        for r0 in range(0, tm, row_chunk):
            rows = slice(r0, r0 + row_chunk)
            acc = o_ref[rows, :]
            h_new = h_ref[rows, :] + acc * _rms_scale(acc) * ga
            o_ref[rows, :] = h_new
            if xn_ref is not None:
                xn_ref[rows, :] = (h_new * _rms_scale(h_new) * gn_ref[...]).astype(BF16)


def _proj_norm_res_last_kernel(a_ref, w_ref, g_ref, h_ref, o_ref, **kw):
    _proj_norm_res_kernel(a_ref, w_ref, g_ref, None, h_ref, o_ref, None, **kw)


def _proj_norm_res(a, w, g, h, alpha, g_next, *, tm=512, tk=512):
    s, kdim = a.shape
    d = w.shape[1]
    row_spec = pl.BlockSpec((tm, d), lambda i, k: (i, 0))
    gain_spec = pl.BlockSpec((1, d), lambda i, k: (0, 0))
    statics = dict(alpha=alpha, col_chunk=1024, row_chunk=64)
    common = dict(grid=(s // tm, kdim // tk), compiler_params=_params(("parallel", "arbitrary")))
    mat_specs = [pl.BlockSpec((tm, tk), lambda i, k: (i, k)), pl.BlockSpec((tk, d), lambda i, k: (k, 0))]
    if g_next is None:
        h_new = pl.pallas_call(
            functools.partial(_proj_norm_res_last_kernel, **statics),
            out_shape=jax.ShapeDtypeStruct((s, d), F32),
            in_specs=mat_specs + [gain_spec, row_spec], out_specs=row_spec,
            name="proj_norm_res_last", **common,
        )(a, w, g.reshape(1, d), h)
        return h_new, None
    return pl.pallas_call(
        functools.partial(_proj_norm_res_kernel, **statics),
        out_shape=(jax.ShapeDtypeStruct((s, d), F32), jax.ShapeDtypeStruct((s, d), BF16)),
        in_specs=mat_specs + [gain_spec, gain_spec, row_spec], out_specs=(row_spec, row_spec),
        name="proj_norm_res", **common,
    )(a, w, g.reshape(1, d), g_next.reshape(1, d), h)


def _in_proj_kernel(u_ref, w_ref, z_ref):
    z_ref[...] = jnp.dot(u_ref[...], w_ref[...], preferred_element_type=F32).astype(BF16)


def _in_proj(u, w, *, tm=1024, tn=1024):
    s, d = u.shape
    n = w.shape[1]
    return pl.pallas_call(
        _in_proj_kernel,
        out_shape=jax.ShapeDtypeStruct((s, n), BF16),
        grid=(s // tm, n // tn),
        in_specs=[
            pl.BlockSpec((tm, d), lambda i, j: (i, 0)),
            pl.BlockSpec((d, tn), lambda i, j: (0, j)),
        ],
        out_specs=pl.BlockSpec((tm, tn), lambda i, j: (i, j)),
        compiler_params=_params(("parallel", "arbitrary")),
        name="in_proj",
    )(u, w)


def _split_bf16(x, parts=3):
    out = []
    for _ in range(parts):
        p = x.astype(BF16)
        out.append(p)
        x = x - p.astype(F32)
    return out


def _forget_kernel(u_ref, wf_ref, bf_ref, tri_ref, c_ref, carry_ref):
    @pl.when(pl.program_id(0) == 0)
    def _():
        carry_ref[...] = jnp.zeros_like(carry_ref)

    f = lax.dot_general(wf_ref[...], u_ref[...], (((1,), (1,)), ((), ())),
                        preferred_element_type=F32)
    x = f + bf_ref[...]
    log_f = jnp.minimum(x, 0.0) - jnp.log(1.0 + jnp.exp(-jnp.abs(x)))
    tri = tri_ref[...]
    cs = sum(jnp.dot(p, tri, preferred_element_type=F32) for p in _split_bf16(log_f))
    c = cs + carry_ref[...]
    c_ref[...] = c
    carry_ref[...] = c[:, c.shape[1] - 1:]


def _forget_cumsum(u, wf_t, bf, *, tm=512):
    s, d = u.shape
    tri = jnp.asarray(np.triu(np.ones((tm, tm), np.float32)), BF16)
    return pl.pallas_call(
        _forget_kernel,
        out_shape=jax.ShapeDtypeStruct((HEAD_ROWS, s), F32),
        grid=(s // tm,),
        in_specs=[
            pl.BlockSpec((tm, d), lambda i: (i, 0)),
            pl.BlockSpec((HEAD_ROWS, d), lambda i: (0, 0)),
            pl.BlockSpec((HEAD_ROWS, 1), lambda i: (0, 0)),
            pl.BlockSpec((tm, tm), lambda i: (0, 0)),
        ],
        out_specs=pl.BlockSpec((HEAD_ROWS, tm), lambda i: (0, i)),
        scratch_shapes=[pltpu.VMEM((HEAD_ROWS, 1), F32)],
        compiler_params=_params(("arbitrary",)),
        name="forget_cumsum",
    )(u, wf_t, bf, tri)


def _lane_groups(x):
    return [x[:, g * LANES:(g + 1) * LANES] for g in range(x.shape[1] // LANES)]


def _max_row_sq_norm(x):
    xf = x.astype(F32)
    return jnp.max(jnp.sum(xf * xf, axis=1, keepdims=True), axis=0, keepdims=True)


def _attn_kernel(q_ref, k_ref, v_ref, c_ref, o_ref, vaug_ref, kn_ref, m_ref, acc_ref, sd_ref, *, tq):
    h = pl.program_id(0)
    i = pl.program_id(1)
    dh = ATT_HEAD_DIM
    seq = k_ref.shape[0]

    @pl.when(i == 0)
    def _():
        vaug_ref[:, 0:dh] = v_ref[...]
        vaug_ref[:, dh:2 * dh] = jnp.ones((seq, dh), BF16)

        def kn_body(t, mx):
            return jnp.maximum(mx, _max_row_sq_norm(k_ref[pl.ds(pl.multiple_of(t * tq, tq), tq), :]))

        kn2 = lax.fori_loop(0, seq // tq, kn_body, jnp.zeros((1, 1), F32))
        kn_ref[...] = jnp.broadcast_to(kn2, kn_ref.shape)

    q = q_ref[...]
    start = pl.multiple_of(i * tq, tq)
    c_q = c_ref[pl.ds(h, 1), pl.ds(start, tq)]
    c_base = c_q[:, 0:1]
    nt_dims = (((1,), (1,)), ((), ()))

    def scores(off, width):
        c_k = c_ref[pl.ds(h, 1), pl.ds(off, width)] - c_base
        return lax.dot_general(q, k_ref[pl.ds(off, width), :], nt_dims, preferred_element_type=F32) - c_k

    def diag_scores():
        rows = lax.broadcasted_iota(jnp.int32, (tq, tq), 0)
        cols = lax.broadcasted_iota(jnp.int32, (tq, tq), 1)
        return jnp.where(cols <= rows, scores(start, tq), NEG_BIG)

    def lane_max(s):
        return functools.reduce(jnp.maximum, _lane_groups(s))

    def probs(s):
        m_rep = m_ref[...]
        return jnp.concatenate([jnp.exp(g - m_rep) for g in _lane_groups(s)], axis=1).astype(BF16)

    def update_max(off, width):
        m_ref[...] = jnp.maximum(m_ref[...], lane_max(scores(off, width)))

    def accumulate(off, width):
        acc_ref[...] += jnp.dot(probs(scores(off, width)), vaug_ref[pl.ds(off, width), :],
                                preferred_element_type=F32)

    bound = jnp.sqrt(_max_row_sq_norm(q) * kn_ref[0:1, 0:1]) * 1.001
    threshold = 2.0 * bound - ATT_STABILISER_MARGIN
    pos = lax.broadcasted_iota(jnp.int32, (1, seq), 1)
    block_end = (pos & (tq - 1)) == (tq - 1)
    near = jnp.where(block_end, c_ref[pl.ds(h, 1), :] - c_base, threshold) < threshold
    n_near = jnp.sum(jnp.where(near & (pos < start), 1.0, 0.0)).astype(jnp.int32)

    sd_ref[...] = diag_scores()
    m_ref[...] = lane_max(sd_ref[...])

    def near_body(t, carry):
        update_max(pl.multiple_of((i - 1 - t) * tq, tq), tq)
        return carry

    lax.fori_loop(0, n_near, near_body, 0)
    m_ref[...] = jnp.broadcast_to(jnp.max(m_ref[...], axis=1, keepdims=True), m_ref.shape)

    acc_ref[...] = jnp.dot(probs(sd_ref[...]), vaug_ref[pl.ds(start, tq), :], preferred_element_type=F32)

    def wide_body(jj, carry):
        accumulate(pl.multiple_of(jj * (2 * tq), 2 * tq), 2 * tq)
        return carry

    lax.fori_loop(0, i // 2, wide_body, 0)

    @pl.when((i & 1) != 0)
    def _():
        accumulate(pl.multiple_of((i - 1) * tq, tq), tq)

    o_ref[...] = (acc_ref[:, 0:dh] / acc_ref[:, dh:dh + 1]).astype(BF16)


def _attention(z, c, *, tq=1024):
    s = z.shape[0]
    dh = ATT_HEAD_DIM
    return pl.pallas_call(
        functools.partial(_attn_kernel, tq=tq),
        out_shape=jax.ShapeDtypeStruct((s, BRANCH_WIDTH), BF16),
        grid=(ATT_HEADS, s // tq),
        in_specs=[
            pl.BlockSpec((tq, dh), lambda h, i: (i, Z_Q // dh + h)),
            pl.BlockSpec((s, dh), lambda h, i: (0, Z_K // dh + h)),
            pl.BlockSpec((s, dh), lambda h, i: (0, Z_V // dh + h)),
            pl.BlockSpec((HEAD_ROWS, s), lambda h, i: (0, 0)),
        ],
        out_specs=pl.BlockSpec((tq, dh), lambda h, i: (i, h)),
        scratch_shapes=[pltpu.VMEM((s, 2 * dh), BF16), pltpu.VMEM((SUBLANES, LANES), F32),
                        pltpu.VMEM((tq, LANES), F32),
                        pltpu.VMEM((tq, 2 * dh), F32),
                        pltpu.VMEM((tq, tq), F32)],
        compiler_params=_params(("parallel", "arbitrary")),
        name="forget_attention",
    )(z, z, z, c)


def _cmul(ar, ai, br, bi):
    return ar * br - ai * bi, ar * bi + ai * br


def _gelu_tanh(x):
    return 0.5 * x * (1.0 + jnp.tanh(math.sqrt(2.0 / math.pi) * (x + 0.044715 * (x * x * x))))


def _ssm_kernel(u_ref, perm_ref, perm_t_ref, b_re_ref, b_im_ref, c_re_ref, c_im_ref,
                lam_ref, lam_sub_ref, pow_sub_ref, pow_step_ref, d_ref, wglu_ref, bglu_ref,
                o_ref, x_re_ref, x_im_ref, cin_ref, carry_ref):
    @pl.when(pl.program_id(0) == 0)
    def _():
        carry_ref[...] = jnp.zeros_like(carry_ref)

    n_sub = SSM_SUB
    n_strips = SSM_STATES // SSM_STRIP
    u_p = jnp.dot(perm_ref[...], u_ref[...], preferred_element_type=F32).astype(BF16)

    for sl in range(SSM_SLABS):
        u_sl = u_p[:, sl * SSM_SLAB_IN:(sl + 1) * SSM_SLAB_IN]
        cols = slice(sl * SSM_SLAB_STATE, (sl + 1) * SSM_SLAB_STATE)
        x_re_ref[:, cols] = jnp.dot(u_sl, b_re_ref[sl], preferred_element_type=F32)
        x_im_ref[:, cols] = jnp.dot(u_sl, b_im_ref[sl], preferred_element_type=F32)

    def scan_strip(st, carry):
        lanes = pl.ds(pl.multiple_of(st * SSM_STRIP, SSM_STRIP), SSM_STRIP)
        lr = jnp.broadcast_to(lam_ref[0:1, lanes], (SUBLANES, SSM_STRIP))
        li = jnp.broadcast_to(lam_ref[1:2, lanes], (SUBLANES, SSM_STRIP))
        sr = x_re_ref[0:SUBLANES, lanes]
        si = x_im_ref[0:SUBLANES, lanes]
        for i in range(1, n_sub):
            rows = slice(i * SUBLANES, (i + 1) * SUBLANES)
            pr, pi = _cmul(lr, li, sr, si)
            sr = pr + x_re_ref[rows, lanes]
            si = pi + x_im_ref[rows, lanes]
            x_re_ref[rows, lanes] = sr
            x_im_ref[rows, lanes] = si
        return carry

    lax.fori_loop(0, n_strips, scan_strip, 0)

    last = slice((n_sub - 1) * SUBLANES, n_sub * SUBLANES)
    tr = x_re_ref[last, :]
    ti = x_im_ref[last, :]
    row = lax.broadcasted_iota(jnp.int32, (SUBLANES, SSM_STATES), 0)
    for lvl, dist in enumerate((1, 2, 4)):
        ar = lam_sub_ref[2 * lvl:2 * lvl + 1, :]
        ai = lam_sub_ref[2 * lvl + 1:2 * lvl + 2, :]
        shr = jnp.where(row >= dist, pltpu.roll(tr, dist, 0), 0.0)
        shi = jnp.where(row >= dist, pltpu.roll(ti, dist, 0), 0.0)
        pr, pi = _cmul(ar, ai, shr, shi)
        tr = tr + pr
        ti = ti + pi
    in_r = carry_ref[0:1, :]
    in_i = carry_ref[1:2, :]
    pr, pi = _cmul(pow_sub_ref[0:SUBLANES, :], pow_sub_ref[SUBLANES:2 * SUBLANES, :], in_r, in_i)
    tr = tr + pr
    ti = ti + pi
    cin_ref[0:SUBLANES, :] = jnp.where(row >= 1, pltpu.roll(tr, 1, 0), in_r)
    cin_ref[SUBLANES:2 * SUBLANES, :] = jnp.where(row >= 1, pltpu.roll(ti, 1, 0), in_i)
    carry_ref[0:1, :] = tr[SUBLANES - 1:SUBLANES, :]
    carry_ref[1:2, :] = ti[SUBLANES - 1:SUBLANES, :]

    def fix_strip(st, carry):
        lanes = pl.ds(pl.multiple_of(st * SSM_STRIP, SSM_STRIP), SSM_STRIP)
        cr = cin_ref[0:SUBLANES, lanes]
        ci = cin_ref[SUBLANES:2 * SUBLANES, lanes]
        for i in range(n_sub):
            rows = slice(i * SUBLANES, (i + 1) * SUBLANES)
            pr, pi = _cmul(pow_step_ref[i:i + 1, lanes], pow_step_ref[n_sub + i:n_sub + i + 1, lanes], cr, ci)
            x_re_ref[rows, lanes] = x_re_ref[rows, lanes] + pr
            x_im_ref[rows, lanes] = x_im_ref[rows, lanes] + pi
        return carry

    lax.fori_loop(0, n_strips, fix_strip, 0)

    ys = []
    for sl in range(SSM_SLABS):
        cols = slice(sl * SSM_SLAB_STATE, (sl + 1) * SSM_SLAB_STATE)
        ys.append(jnp.dot(x_re_ref[:, cols].astype(BF16), c_re_ref[sl], preferred_element_type=F32)
                  - jnp.dot(x_im_ref[:, cols].astype(BF16), c_im_ref[sl], preferred_element_type=F32))
    y = jnp.concatenate(ys, axis=1) + d_ref[...] * u_p.astype(F32)
    y = _gelu_tanh(y)
    gate = jnp.dot(y.astype(BF16), wglu_ref[...], preferred_element_type=F32) + bglu_ref[...]
    y = (y * _sigmoid(gate)).astype(BF16)
    o_ref[...] = jnp.dot(perm_t_ref[...], y, preferred_element_type=F32).astype(BF16)


def _ssm_tables(lam_re, lam_im, log_dt, b_re, b_im, c_re, c_im):
    dt = jnp.exp(log_dt)[:, None]
    a_re = lam_re * dt
    a_im = lam_im * dt

    def lam_pow(n):
        mag = jnp.exp(a_re * n)
        return (mag * jnp.cos(a_im * n)).reshape(-1), (mag * jnp.sin(a_im * n)).reshape(-1)

    lb_re = jnp.exp(a_re) * jnp.cos(a_im)
    lb_im = jnp.exp(a_re) * jnp.sin(a_im)
    den = lam_re * lam_re + lam_im * lam_im
    k_re = ((lb_re - 1.0) * lam_re + lb_im * lam_im) / den
    k_im = (lb_im * lam_re - (lb_re - 1.0) * lam_im) / den
    bb_re = k_re[..., None] * b_re - k_im[..., None] * b_im
    bb_im = k_re[..., None] * b_im + k_im[..., None] * b_re

    g, p, hg = b_re.shape
    gs = g // SSM_SLABS
    eye = jnp.eye(gs, dtype=F32)

    def b_slabs(x):
        x = x.reshape(SSM_SLABS, gs, p, hg)
        return jnp.einsum('sgph,gk->sghkp', x, eye).reshape(SSM_SLABS, gs * hg, gs * p).astype(BF16)

    def c_slabs(x):
        x = x.reshape(SSM_SLABS, gs, hg, p)
        return jnp.einsum('sghp,gk->sgpkh', x, eye).reshape(SSM_SLABS, gs * p, gs * hg).astype(BF16)

    lam = jnp.stack(lam_pow(1.0))
    lam_sub = jnp.concatenate([jnp.stack(lam_pow(float(SSM_SUB * d))) for d in (1, 2, 4)])
    sub_pows = [lam_pow(float(SSM_SUB * (j + 1))) for j in range(SUBLANES)]
    pow_sub = jnp.stack([x[0] for x in sub_pows] + [x[1] for x in sub_pows])
    step_pows = [lam_pow(float(i + 1)) for i in range(SSM_SUB)]
    pow_step = jnp.stack([x[0] for x in step_pows] + [x[1] for x in step_pows])
    return b_slabs(bb_re), b_slabs(bb_im), c_slabs(c_re), c_slabs(c_im), lam, lam_sub, pow_sub, pow_step


def _ssm_permutation():
    perm = np.zeros((SSM_CHUNK, SSM_CHUNK), np.float32)
    for j in range(SUBLANES):
        for i in range(SSM_SUB):
            perm[i * SUBLANES + j, j * SSM_SUB + i] = 1.0
    return jnp.asarray(perm, BF16), jnp.asarray(perm.T, BF16)


def _ssm(z, tables, d_skip, w_glu, b_glu):
    s = z.shape[0]
    w = BRANCH_WIDTH
    perm, perm_t = _ssm_permutation()
    b_re, b_im, c_re, c_im, lam, lam_sub, pow_sub, pow_step = tables
    full = lambda a: pl.BlockSpec(a.shape, lambda c, _n=a.ndim: (0,) * _n)
    operands = (perm, perm_t, b_re, b_im, c_re, c_im, lam, lam_sub, pow_sub, pow_step,
                d_skip.reshape(1, w), w_glu, b_glu.reshape(1, w))
    return pl.pallas_call(
        _ssm_kernel,
        out_shape=jax.ShapeDtypeStruct((s, w), BF16),
        grid=(s // SSM_CHUNK,),
        in_specs=[pl.BlockSpec((SSM_CHUNK, w), lambda c: (c, Z_SSM // w))] + [full(a) for a in operands],
        out_specs=pl.BlockSpec((SSM_CHUNK, w), lambda c: (c, 0)),
        scratch_shapes=[
            pltpu.VMEM((SSM_CHUNK, SSM_STATES), F32),
            pltpu.VMEM((SSM_CHUNK, SSM_STATES), F32),
            pltpu.VMEM((2 * SUBLANES, SSM_STATES), F32),
            pltpu.VMEM((2, SSM_STATES), F32),
        ],
        compiler_params=_params(("arbitrary",)),
        name="s5_ssm",
    )(z, *operands)


def _conv_kernel(val_ref, gate_ref, hval_ref, hgate_ref, w_ref, b_ref, lng_ref, lnb_ref, o_ref,
                 a_ref, sh_ref, y_ref, *, tt, row_chunk, lane_chunk):
    a_ref[CONV_HALO:CONV_HALO + tt, :] = val_ref[...].astype(F32) * _sigmoid(gate_ref[...].astype(F32))
    halo = hval_ref[...].astype(F32) * _sigmoid(hgate_ref[...].astype(F32))
    a_ref[0:CONV_HALO, :] = jnp.where(pl.program_id(0) > 0, halo, 0.0)

    n_shifted = tt + CONV_HALO - SUBLANES
    for r in range(1, SUBLANES):
        sh_ref[r - 1, 0:n_shifted, :] = a_ref[r:r + n_shifted, :]

    first = CONV_HALO - (CONV_WIDTH - 1)
    for c0 in range(0, BRANCH_WIDTH, lane_chunk):
        lanes = slice(c0, c0 + lane_chunk)
        for r0 in range(0, tt, row_chunk):
            acc = jnp.broadcast_to(b_ref[:, lanes], (row_chunk, lane_chunk))
            for j in range(CONV_WIDTH):
                tiles, r = divmod(first + j, SUBLANES)
                rows = slice(r0 + tiles * SUBLANES, r0 + tiles * SUBLANES + row_chunk)
                tap = a_ref[rows, lanes] if r == 0 else sh_ref[r - 1, rows, lanes]
                acc = acc + w_ref[j:j + 1, lanes] * tap
            y_ref[r0:r0 + row_chunk, lanes] = acc

    y = y_ref[...]
    mu = jnp.mean(y, axis=-1, keepdims=True)
    yc = y - mu
    var = jnp.mean(yc * yc, axis=-1, keepdims=True)
    yn = yc * lax.rsqrt(var + EPS) * lng_ref[...] + lnb_ref[...]
    o_ref[...] = (yn * _sigmoid(yn)).astype(BF16)


def _conv(z, w_dw, b_dw, ln_g, ln_b, *, tt=256):
    s = z.shape[0]
    w = BRANCH_WIDTH
    hb = tt // CONV_HALO
    halo_map = lambda col: (lambda i: (jnp.maximum(i * hb - 1, 0), col))
    return pl.pallas_call(
        functools.partial(_conv_kernel, tt=tt, row_chunk=64, lane_chunk=256),
        out_shape=jax.ShapeDtypeStruct((s, w), BF16),
        grid=(s // tt,),
        in_specs=[
            pl.BlockSpec((tt, w), lambda i: (i, Z_CVAL // w)),
            pl.BlockSpec((tt, w), lambda i: (i, Z_CGATE // w)),
            pl.BlockSpec((CONV_HALO, w), halo_map(Z_CVAL // w)),
            pl.BlockSpec((CONV_HALO, w), halo_map(Z_CGATE // w)),
            pl.BlockSpec((CONV_WIDTH, w), lambda i: (0, 0)),
            pl.BlockSpec((1, w), lambda i: (0, 0)),
            pl.BlockSpec((1, w), lambda i: (0, 0)),
            pl.BlockSpec((1, w), lambda i: (0, 0)),
        ],
        out_specs=pl.BlockSpec((tt, w), lambda i: (i, 0)),
        scratch_shapes=[pltpu.VMEM((CONV_HALO + tt, w), F32),
                        pltpu.VMEM((SUBLANES - 1, CONV_HALO + tt, w), F32),
                        pltpu.VMEM((tt, w), F32)],
        compiler_params=_params(("parallel",)),
        name="conformer_conv",
    )(z, z, z, z, w_dw, b_dw.reshape(1, w), ln_g.reshape(1, w), ln_b.reshape(1, w))


def _pool_kernel(p_ref, hp_ref, w_ref, b_ref, sc_ref, o_ref, e_ref, *, tt):
    i = pl.program_id(0)
    e_ref[POOL_HALO:POOL_HALO + tt, :] = p_ref[...].astype(F32)
    e_ref[0:POOL_HALO, :] = jnp.where(i > 0, hp_ref[...].astype(F32), 0.0)
    pos = (i * tt + 1 + lax.broadcasted_iota(jnp.int32, (tt, 1), 0)).astype(F32)
    for g, win in enumerate(POOL_WINDOWS):
        lanes = slice(g * POOL_GROUP, (g + 1) * POOL_GROUP)
        cur = e_ref[POOL_HALO:POOL_HALO + tt, lanes]
        tot = cur
        for lag in range(1, win):
            tot = tot + e_ref[POOL_HALO - lag:POOL_HALO - lag + tt, lanes]
        zg = tot / jnp.minimum(pos, float(win)) - cur
        out = jnp.dot(zg.astype(BF16), w_ref[g], preferred_element_type=F32) + b_ref[g:g + 1, :]
        o_ref[:, lanes] = (out * sc_ref[:, lanes]).astype(BF16)


def _pool(z, w_pool, b_pool, scale, *, tt=512):
    s = z.shape[0]
    w = BRANCH_WIDTH
    hb = tt // POOL_HALO
    return pl.pallas_call(
        functools.partial(_pool_kernel, tt=tt),
        out_shape=jax.ShapeDtypeStruct((s, w), BF16),
        grid=(s // tt,),
        in_specs=[
            pl.BlockSpec((tt, w), lambda i: (i, Z_POOL // w)),
            pl.BlockSpec((POOL_HALO, w), lambda i: (jnp.maximum(i * hb - 1, 0), Z_POOL // w)),
            pl.BlockSpec(w_pool.shape, lambda i: (0, 0, 0)),
            pl.BlockSpec(b_pool.shape, lambda i: (0, 0)),
            pl.BlockSpec((1, w), lambda i: (0, 0)),
        ],
        out_specs=pl.BlockSpec((tt, w), lambda i: (i, 0)),
        scratch_shapes=[pltpu.VMEM((POOL_HALO + tt, w), F32)],
        compiler_params=_params(("parallel",)),
        name="multiscale_pool",
    )(z, z, w_pool, b_pool, scale.reshape(1, w))


def _merge_kernel(u_ref, ya_ref, yb_ref, yc_ref, yd_ref, gw_ref, gb_ref, up_ref, o_ref):
    u = u_ref[...]
    acc = None
    for n, y_ref in enumerate((ya_ref, yb_ref, yc_ref, yd_ref)):
        gate = _sigmoid(jnp.dot(u, gw_ref[n], preferred_element_type=F32) + gb_ref[n:n + 1, :])
        term = gate * jnp.dot(y_ref[...], up_ref[n], preferred_element_type=F32)
        acc = term if acc is None else acc + term
    o_ref[...] = acc.astype(BF16)


def _merge(u, ys, gate_w, gate_b, up_w, *, tm=1024):
    s, d = u.shape
    nb, w = up_w.shape[0], up_w.shape[1]
    gb = GATE_BLOCK
    y_spec = pl.BlockSpec((tm, w), lambda i, k: (i, 0))
    return pl.pallas_call(
        _merge_kernel,
        out_shape=jax.ShapeDtypeStruct((s, d), BF16),
        grid=(s // tm, d // gb),
        in_specs=[
            pl.BlockSpec((tm, gb), lambda i, k: (i, k)),
            y_spec, y_spec, y_spec, y_spec,
            pl.BlockSpec((nb, None, gb, gb), lambda i, k: (0, k, 0, 0)),
            pl.BlockSpec((nb, gb), lambda i, k: (0, k)),
            pl.BlockSpec((nb, w, gb), lambda i, k: (0, 0, k)),
        ],
        out_specs=pl.BlockSpec((tm, gb), lambda i, k: (i, k)),
        compiler_params=_params(("parallel", "arbitrary")),
        name="gated_merge",
    )(u, *ys, gate_w, gate_b, up_w)


def _ffn_block(h, xn, w_gate, w_up, w_down, post_g, g_next):
    act = _ffn_up(xn, w_gate.astype(BF16), w_up.astype(BF16))
    return _proj_norm_res(act, w_down.astype(BF16), post_g, h, 0.5, g_next)


def _mixer_block(h, u, w_in, forget_bias, ssm_params, ssm_d, ssm_w_glu, ssm_b_glu,
                 conv_w, conv_b, conv_ln_g, conv_ln_b, pool_w, pool_b, pool_scale,
                 branch_w_up, gate_w, gate_b, w_out, post_g, g_next):
    off_f = 3 * BRANCH_WIDTH
    w_q = w_in[:, :BRANCH_WIDTH] * (ATT_HEAD_DIM ** -0.5)
    w_main = jnp.concatenate([w_q, w_in[:, BRANCH_WIDTH:off_f], w_in[:, off_f + ATT_HEADS:]], axis=1).astype(BF16)
    w_f = jnp.zeros((HEAD_ROWS, D_MODEL), F32).at[:ATT_HEADS].set(w_in[:, off_f:off_f + ATT_HEADS].T).astype(BF16)
    b_f = jnp.zeros((HEAD_ROWS, 1), F32).at[:ATT_HEADS, 0].set(forget_bias)

    z = _in_proj(u, w_main)
    c = _forget_cumsum(u, w_f, b_f)
    y_a = _attention(z, c)
    y_b = _ssm(z, _ssm_tables(*ssm_params), ssm_d, ssm_w_glu.astype(BF16), ssm_b_glu)
    y_c = _conv(z, conv_w, conv_b, conv_ln_g, conv_ln_b)
    y_d = _pool(z, pool_w.astype(BF16), pool_b, pool_scale)
    merged = _merge(u, (y_a, y_b, y_c, y_d), gate_w.astype(BF16), gate_b, branch_w_up.astype(BF16))
    return _proj_norm_res(merged, w_out.astype(BF16), post_g, h, 1.0, g_next)


def kernel(x, ffn1_pre_g, ffn1_w_gate, ffn1_w_up, ffn1_w_down, ffn1_post_g, mix_pre_g, w_in, forget_bias, ssm_lam_re, ssm_lam_im, ssm_log_dt, ssm_b_re, ssm_b_im, ssm_c_re, ssm_c_im, ssm_d, ssm_w_glu, ssm_b_glu, conv_w, conv_b, conv_ln_g, conv_ln_b, pool_w, pool_b, pool_scale, branch_w_up, gate_w, gate_b, w_out, mix_post_g, ffn2_pre_g, ffn2_w_gate, ffn2_w_up, ffn2_w_down, ffn2_post_g):
    batch, seq, d = x.shape
    outs = []
    for b in range(batch):
        h = x[b]
        depth = ffn1_pre_g.shape[0]
        xn = _rms_norm(h, ffn1_pre_g[0])
        for l in range(depth):
            h, xn = _ffn_block(h, xn, ffn1_w_gate[l], ffn1_w_up[l], ffn1_w_down[l], ffn1_post_g[l], mix_pre_g[l])
            h, xn = _mixer_block(
                h, xn, w_in[l], forget_bias[l],
                (ssm_lam_re[l], ssm_lam_im[l], ssm_log_dt[l], ssm_b_re[l], ssm_b_im[l], ssm_c_re[l], ssm_c_im[l]),
                ssm_d[l], ssm_w_glu[l], ssm_b_glu[l], conv_w[l], conv_b[l], conv_ln_g[l], conv_ln_b[l],
                pool_w[l], pool_b[l], pool_scale[l], branch_w_up[l], gate_w[l], gate_b[l], w_out[l], mix_post_g[l],
                ffn2_pre_g[l])
            g_next = ffn1_pre_g[l + 1] if l + 1 < depth else None
            h, xn = _ffn_block(h, xn, ffn2_w_gate[l], ffn2_w_up[l], ffn2_w_down[l], ffn2_post_g[l], g_next)
        outs.append(h)
    return jnp.stack(outs)
```

```python
import functools
import math

import numpy as np
import jax
import jax.numpy as jnp
from jax import lax
from jax.experimental import pallas as pl
from jax.experimental.pallas import tpu as pltpu

F32 = jnp.float32
BF16 = jnp.bfloat16

D_MODEL = 4096
BRANCH_WIDTH = 1024
ATT_HEADS = 8
ATT_HEAD_DIM = 128
SSM_GROUP = 16
SSM_GROUPS = 64
SSM_STATE = 64
CONV_WIDTH = 31
POOL_WINDOWS = (2, 4, 8, 16)
POOL_GROUP = 256
GATE_BLOCKS = 16
GATE_BLOCK = 256
D_FF = 6144
EPS = 1e-6

V7X_VMEM_BYTES = 64 * 1024 * 1024
VMEM_LIMIT_BYTES = V7X_VMEM_BYTES - 8 * 1024 * 1024
SUBLANES = 8
LANES = 128

Z_Q, Z_K, Z_V, Z_SSM, Z_CVAL, Z_CGATE, Z_POOL = (n * BRANCH_WIDTH for n in range(7))
Z_WIDTH = 7 * BRANCH_WIDTH
HEAD_ROWS = 16

SSM_CHUNK = 256
SSM_SUB = SSM_CHUNK // SUBLANES
SSM_SLABS = 4
SSM_SLAB_IN = BRANCH_WIDTH // SSM_SLABS
SSM_STATES = SSM_GROUPS * SSM_STATE
SSM_SLAB_STATE = SSM_STATES // SSM_SLABS
SSM_STRIP = 512

CONV_HALO = 32
POOL_HALO = 16

NEG_BIG = -1e30
ATT_STABILISER_MARGIN = 60.0


def _params(semantics):
    return pltpu.CompilerParams(dimension_semantics=semantics, vmem_limit_bytes=VMEM_LIMIT_BYTES)


def _sigmoid(x):
    return 1.0 / (1.0 + jnp.exp(-x))


def _rms_scale(x):
    return lax.rsqrt(jnp.mean(x * x, axis=-1, keepdims=True) + EPS)


def _rms_norm_kernel(x_ref, g_ref, o_ref):
    x = x_ref[...]
    o_ref[...] = (x * _rms_scale(x) * g_ref[...]).astype(BF16)


def _rms_norm(x, g, *, tm=256):
    s, d = x.shape
    return pl.pallas_call(
        _rms_norm_kernel,
        out_shape=jax.ShapeDtypeStruct((s, d), BF16),
        grid=(s // tm,),
        in_specs=[pl.BlockSpec((tm, d), lambda i: (i, 0)), pl.BlockSpec((1, d), lambda i: (0, 0))],
        out_specs=pl.BlockSpec((tm, d), lambda i: (i, 0)),
        compiler_params=_params(("parallel",)),
        name="rms_norm",
    )(x, g.reshape(1, d))


def _ffn_up_kernel(x_ref, wg_ref, wu_ref, o_ref):
    x = x_ref[...]
    a = jnp.dot(x, wg_ref[...], preferred_element_type=F32)
    b = jnp.dot(x, wu_ref[...], preferred_element_type=F32)
    o_ref[...] = (a * _sigmoid(a) * b).astype(BF16)


def _ffn_up(xn, wg, wu, *, tm=1024, tn=512):
    s, d = xn.shape
    f = wg.shape[1]
    return pl.pallas_call(
        _ffn_up_kernel,
        out_shape=jax.ShapeDtypeStruct((s, f), BF16),
        grid=(s // tm, f // tn),
        in_specs=[
            pl.BlockSpec((tm, d), lambda i, j: (i, 0)),
            pl.BlockSpec((d, tn), lambda i, j: (0, j)),
            pl.BlockSpec((d, tn), lambda i, j: (0, j)),
        ],
        out_specs=pl.BlockSpec((tm, tn), lambda i, j: (i, j)),
        compiler_params=_params(("parallel", "arbitrary")),
        name="ffn_up",
    )(xn, wg, wu)


PROJ_DMA_STEP = 2


def _proj_norm_res_kernel(*refs, alpha, tm, col_chunk, row_chunk, emit_xn):
    if emit_xn:
        a_ref, w_ref, g_ref, gn_ref, h_hbm, ho_hbm, xn_hbm, acc_ref, hbuf_ref, xnbuf_ref, sems = refs
    else:
        a_ref, w_ref, g_ref, h_hbm, ho_hbm, acc_ref, hbuf_ref, sems = refs
    i = pl.program_id(0)
    k = pl.program_id(1)
    n_i = pl.num_programs(0)
    n_k = pl.num_programs(1)
    d = acc_ref.shape[1]

    def block_rows(ref, blk):
        return ref.at[pl.ds(pl.multiple_of(blk * tm, tm), tm), :]

    def h_in_copy(blk):
        return pltpu.make_async_copy(block_rows(h_hbm, blk), hbuf_ref, sems.at[0])

    def result_copies(blk):
        copies = [pltpu.make_async_copy(hbuf_ref, block_rows(ho_hbm, blk), sems.at[1])]
        if emit_xn:
            copies.append(pltpu.make_async_copy(xnbuf_ref, block_rows(xn_hbm, blk), sems.at[2]))
        return copies

    def partial_products(accumulate):
        a = a_ref[...]
        for c0 in range(0, d, col_chunk):
            cols = slice(c0, c0 + col_chunk)
            t = jnp.dot(a, w_ref[:, cols], preferred_element_type=F32)
            if accumulate:
                acc_ref[:, cols] += t
            else:
                acc_ref[:, cols] = t

    @pl.when(k == 0)
    def _():
        partial_products(False)

    @pl.when(k > 0)
    def _():
        partial_products(True)

    @pl.when(k == PROJ_DMA_STEP)
    def _():
        @pl.when(i > 0)
        def _():
            for cp in result_copies(i - 1):
                cp.wait()

        h_in_copy(i).start()

    @pl.when(k == n_k - 1)
    def _():
        h_in_copy(i).wait()
        ga = alpha * g_ref[...]
        for r0 in range(0, tm, row_chunk):
            rows = slice(r0, r0 + row_chunk)
            acc = acc_ref[rows, :]
            h_new = hbuf_ref[rows, :] + acc * _rms_scale(acc) * ga
            hbuf_ref[rows, :] = h_new
            if emit_xn:
                xnbuf_ref[rows, :] = (h_new * _rms_scale(h_new) * gn_ref[...]).astype(BF16)
        for cp in result_copies(i):
            cp.start()

        @pl.when(i == n_i - 1)
        def _():
            for cp in result_copies(i):
                cp.wait()


def _proj_norm_res(a, w, g, h, alpha, g_next, *, tm=1024, tk=512):
    s, kdim = a.shape
    d = w.shape[1]
    emit_xn = g_next is not None
    assert kdim // tk > PROJ_DMA_STEP, "the copy schedule needs more than PROJ_DMA_STEP contraction steps"
    gain_spec = pl.BlockSpec((1, d), lambda i, k: (0, 0))
    any_spec = pl.BlockSpec(memory_space=pl.ANY)
    gains = [g.reshape(1, d)] + ([g_next.reshape(1, d)] if emit_xn else [])
    out_shape = [jax.ShapeDtypeStruct((s, d), F32)] + ([jax.ShapeDtypeStruct((s, d), BF16)] if emit_xn else [])
    scratch = [pltpu.VMEM((tm, d), F32), pltpu.VMEM((tm, d), F32)]
    scratch += [pltpu.VMEM((tm, d), BF16)] if emit_xn else []
    scratch += [pltpu.SemaphoreType.DMA((3,))]
    outs = pl.pallas_call(
        functools.partial(_proj_norm_res_kernel, alpha=alpha, tm=tm, col_chunk=512, row_chunk=64, emit_xn=emit_xn),
        out_shape=out_shape,
        grid=(s // tm, kdim // tk),
        in_specs=[pl.BlockSpec((tm, tk), lambda i, k: (i, k)), pl.BlockSpec((tk, d), lambda i, k: (k, 0))]
        + [gain_spec] * len(gains) + [any_spec],
        out_specs=[any_spec] * len(out_shape),
        scratch_shapes=scratch,
        compiler_params=_params(("arbitrary", "arbitrary")),
        name="proj_norm_res" if emit_xn else "proj_norm_res_last",
    )(a, w, *gains, h)
    return (outs[0], outs[1]) if emit_xn else (outs[0], None)


def _in_proj_kernel(u_ref, w_ref, z_ref):
    z_ref[...] = jnp.dot(u_ref[...], w_ref[...], preferred_element_type=F32).astype(BF16)


def _in_proj(u, w, *, tm=1024, tn=1024):
    s, d = u.shape
    n = w.shape[1]
    return pl.pallas_call(
        _in_proj_kernel,
        out_shape=jax.ShapeDtypeStruct((s, n), BF16),
        grid=(s // tm, n // tn),
        in_specs=[
            pl.BlockSpec((tm, d), lambda i, j: (i, 0)),
            pl.BlockSpec((d, tn), lambda i, j: (0, j)),
        ],
        out_specs=pl.BlockSpec((tm, tn), lambda i, j: (i, j)),
        compiler_params=_params(("parallel", "arbitrary")),
        name="in_proj",
    )(u, w)


def _split_bf16(x, parts=3):
    out = []
    for _ in range(parts):
        p = x.astype(BF16)
        out.append(p)
        x = x - p.astype(F32)
    return out


def _forget_kernel(u_ref, wf_ref, bf_ref, tri_ref, c_ref, carry_ref):
    @pl.when(pl.program_id(0) == 0)
    def _():
        carry_ref[...] = jnp.zeros_like(carry_ref)

    f = lax.dot_general(wf_ref[...], u_ref[...], (((1,), (1,)), ((), ())),
                        preferred_element_type=F32)
    x = f + bf_ref[...]
    log_f = jnp.minimum(x, 0.0) - jnp.log(1.0 + jnp.exp(-jnp.abs(x)))
    tri = tri_ref[...]
    cs = sum(jnp.dot(p, tri, preferred_element_type=F32) for p in _split_bf16(log_f))
    c = cs + carry_ref[...]
    c_ref[...] = c
    carry_ref[...] = c[:, c.shape[1] - 1:]


def _forget_cumsum(u, wf_t, bf, *, tm=512):
    s, d = u.shape
    tri = jnp.asarray(np.triu(np.ones((tm, tm), np.float32)), BF16)
    return pl.pallas_call(
        _forget_kernel,
        out_shape=jax.ShapeDtypeStruct((HEAD_ROWS, s), F32),
        grid=(s // tm,),
        in_specs=[
            pl.BlockSpec((tm, d), lambda i: (i, 0)),
            pl.BlockSpec((HEAD_ROWS, d), lambda i: (0, 0)),
            pl.BlockSpec((HEAD_ROWS, 1), lambda i: (0, 0)),
            pl.BlockSpec((tm, tm), lambda i: (0, 0)),
        ],
        out_specs=pl.BlockSpec((HEAD_ROWS, tm), lambda i: (0, i)),
        scratch_shapes=[pltpu.VMEM((HEAD_ROWS, 1), F32)],
        compiler_params=_params(("arbitrary",)),
        name="forget_cumsum",
    )(u, wf_t, bf, tri)


def _lane_groups(x):
    return [x[:, g * LANES:(g + 1) * LANES] for g in range(x.shape[1] // LANES)]


def _max_row_sq_norm(x):
    xf = x.astype(F32)
    return jnp.max(jnp.sum(xf * xf, axis=1, keepdims=True), axis=0, keepdims=True)


def _attn_kernel(q_ref, k_ref, v_ref, c_ref, o_ref, vaug_ref, kn_ref, m_ref, acc_ref, sd_ref, *, tq):
    h = pl.program_id(0)
    i = pl.program_id(1)
    dh = ATT_HEAD_DIM
    seq = k_ref.shape[0]

    @pl.when(i == 0)
    def _():
        vaug_ref[:, 0:dh] = v_ref[...]
        vaug_ref[:, dh:2 * dh] = jnp.ones((seq, dh), BF16)

        def kn_body(t, mx):
            return jnp.maximum(mx, _max_row_sq_norm(k_ref[pl.ds(pl.multiple_of(t * tq, tq), tq), :]))

        kn2 = lax.fori_loop(0, seq // tq, kn_body, jnp.zeros((1, 1), F32))
        kn_ref[...] = jnp.broadcast_to(kn2, kn_ref.shape)

    q = q_ref[...]
    start = pl.multiple_of(i * tq, tq)
    c_q = c_ref[pl.ds(h, 1), pl.ds(start, tq)]
    c_base = c_q[:, 0:1]
    nt_dims = (((1,), (1,)), ((), ()))

    def scores(off, width):
        c_k = c_ref[pl.ds(h, 1), pl.ds(off, width)] - c_base
        return lax.dot_general(q, k_ref[pl.ds(off, width), :], nt_dims, preferred_element_type=F32) - c_k

    def diag_scores():
        rows = lax.broadcasted_iota(jnp.int32, (tq, tq), 0)
        cols = lax.broadcasted_iota(jnp.int32, (tq, tq), 1)
        return jnp.where(cols <= rows, scores(start, tq), NEG_BIG)

    def lane_max(s):
        return functools.reduce(jnp.maximum, _lane_groups(s))

    def probs(s):
        m_rep = m_ref[...]
        return jnp.concatenate([jnp.exp(g - m_rep) for g in _lane_groups(s)], axis=1).astype(BF16)

    def update_max(off, width):
        m_ref[...] = jnp.maximum(m_ref[...], lane_max(scores(off, width)))

    def accumulate(off, width):
        acc_ref[...] += jnp.dot(probs(scores(off, width)), vaug_ref[pl.ds(off, width), :],
                                preferred_element_type=F32)

    bound = jnp.sqrt(_max_row_sq_norm(q) * kn_ref[0:1, 0:1]) * 1.001
    threshold = 2.0 * bound - ATT_STABILISER_MARGIN
    pos = lax.broadcasted_iota(jnp.int32, (1, seq), 1)
    block_end = (pos & (tq - 1)) == (tq - 1)
    near = jnp.where(block_end, c_ref[pl.ds(h, 1), :] - c_base, threshold) < threshold
    n_near = jnp.sum(jnp.where(near & (pos < start), 1.0, 0.0)).astype(jnp.int32)

    sd_ref[...] = diag_scores()
    m_ref[...] = lane_max(sd_ref[...])

    def near_body(t, carry):
        update_max(pl.multiple_of((i - 1 - t) * tq, tq), tq)
        return carry

    lax.fori_loop(0, n_near, near_body, 0)
    m_ref[...] = jnp.broadcast_to(jnp.max(m_ref[...], axis=1, keepdims=True), m_ref.shape)

    acc_ref[...] = jnp.dot(probs(sd_ref[...]), vaug_ref[pl.ds(start, tq), :], preferred_element_type=F32)

    def wide_body(jj, carry):
        accumulate(pl.multiple_of(jj * (2 * tq), 2 * tq), 2 * tq)
        return carry

    lax.fori_loop(0, i // 2, wide_body, 0)

    @pl.when((i & 1) != 0)
    def _():
        accumulate(pl.multiple_of((i - 1) * tq, tq), tq)

    o_ref[...] = (acc_ref[:, 0:dh] / acc_ref[:, dh:dh + 1]).astype(BF16)


def _attention(z, c, *, tq=1024):
    s = z.shape[0]
    dh = ATT_HEAD_DIM
    return pl.pallas_call(
        functools.partial(_attn_kernel, tq=tq),
        out_shape=jax.ShapeDtypeStruct((s, BRANCH_WIDTH), BF16),
        grid=(ATT_HEADS, s // tq),
        in_specs=[
            pl.BlockSpec((tq, dh), lambda h, i: (i, Z_Q // dh + h)),
            pl.BlockSpec((s, dh), lambda h, i: (0, Z_K // dh + h)),
            pl.BlockSpec((s, dh), lambda h, i: (0, Z_V // dh + h)),
            pl.BlockSpec((HEAD_ROWS, s), lambda h, i: (0, 0)),
        ],
        out_specs=pl.BlockSpec((tq, dh), lambda h, i: (i, h)),
        scratch_shapes=[pltpu.VMEM((s, 2 * dh), BF16), pltpu.VMEM((SUBLANES, LANES), F32),
                        pltpu.VMEM((tq, LANES), F32),
                        pltpu.VMEM((tq, 2 * dh), F32),
                        pltpu.VMEM((tq, tq), F32)],
        compiler_params=_params(("parallel", "arbitrary")),
        name="forget_attention",
    )(z, z, z, c)


def _cmul(ar, ai, br, bi):
    return ar * br - ai * bi, ar * bi + ai * br


def _gelu_tanh(x):
    return 0.5 * x * (1.0 + jnp.tanh(math.sqrt(2.0 / math.pi) * (x + 0.044715 * (x * x * x))))


def _ssm_kernel(u_ref, perm_ref, perm_t_ref, b_re_ref, b_im_ref, c_re_ref, c_im_ref,
                lam_ref, lam_sub_ref, pow_sub_ref, pow_step_ref, d_ref, wglu_ref, bglu_ref,
                o_ref, x_re_ref, x_im_ref, cin_ref, carry_ref):
    @pl.when(pl.program_id(0) == 0)
    def _():
        carry_ref[...] = jnp.zeros_like(carry_ref)

    n_sub = SSM_SUB
    n_strips = SSM_STATES // SSM_STRIP
    u_p = jnp.dot(perm_ref[...], u_ref[...], preferred_element_type=F32).astype(BF16)

    for sl in range(SSM_SLABS):
        u_sl = u_p[:, sl * SSM_SLAB_IN:(sl + 1) * SSM_SLAB_IN]
        cols = slice(sl * SSM_SLAB_STATE, (sl + 1) * SSM_SLAB_STATE)
        x_re_ref[:, cols] = jnp.dot(u_sl, b_re_ref[sl], preferred_element_type=F32)
        x_im_ref[:, cols] = jnp.dot(u_sl, b_im_ref[sl], preferred_element_type=F32)

    def scan_strip(st, carry):
        lanes = pl.ds(pl.multiple_of(st * SSM_STRIP, SSM_STRIP), SSM_STRIP)
        lr = jnp.broadcast_to(lam_ref[0:1, lanes], (SUBLANES, SSM_STRIP))
        li = jnp.broadcast_to(lam_ref[1:2, lanes], (SUBLANES, SSM_STRIP))
        sr = x_re_ref[0:SUBLANES, lanes]
        si = x_im_ref[0:SUBLANES, lanes]
        for i in range(1, n_sub):
            rows = slice(i * SUBLANES, (i + 1) * SUBLANES)
            pr, pi = _cmul(lr, li, sr, si)
            sr = pr + x_re_ref[rows, lanes]
            si = pi + x_im_ref[rows, lanes]
            x_re_ref[rows, lanes] = sr
            x_im_ref[rows, lanes] = si
        return carry

    lax.fori_loop(0, n_strips, scan_strip, 0)

    last = slice((n_sub - 1) * SUBLANES, n_sub * SUBLANES)
    tr = x_re_ref[last, :]
    ti = x_im_ref[last, :]
    row = lax.broadcasted_iota(jnp.int32, (SUBLANES, SSM_STATES), 0)
    for lvl, dist in enumerate((1, 2, 4)):
        ar = lam_sub_ref[2 * lvl:2 * lvl + 1, :]
        ai = lam_sub_ref[2 * lvl + 1:2 * lvl + 2, :]
        shr = jnp.where(row >= dist, pltpu.roll(tr, dist, 0), 0.0)
        shi = jnp.where(row >= dist, pltpu.roll(ti, dist, 0), 0.0)
        pr, pi = _cmul(ar, ai, shr, shi)
        tr = tr + pr
        ti = ti + pi
    in_r = carry_ref[0:1, :]
    in_i = carry_ref[1:2, :]
    pr, pi = _cmul(pow_sub_ref[0:SUBLANES, :], pow_sub_ref[SUBLANES:2 * SUBLANES, :], in_r, in_i)
    tr = tr + pr
    ti = ti + pi
    cin_ref[0:SUBLANES, :] = jnp.where(row >= 1, pltpu.roll(tr, 1, 0), in_r)
    cin_ref[SUBLANES:2 * SUBLANES, :] = jnp.where(row >= 1, pltpu.roll(ti, 1, 0), in_i)
    carry_ref[0:1, :] = tr[SUBLANES - 1:SUBLANES, :]
    carry_ref[1:2, :] = ti[SUBLANES - 1:SUBLANES, :]

    def fix_strip(st, carry):
        lanes = pl.ds(pl.multiple_of(st * SSM_STRIP, SSM_STRIP), SSM_STRIP)
        cr = cin_ref[0:SUBLANES, lanes]
        ci = cin_ref[SUBLANES:2 * SUBLANES, lanes]
        for i in range(n_sub):
            rows = slice(i * SUBLANES, (i + 1) * SUBLANES)
            pr, pi = _cmul(pow_step_ref[i:i + 1, lanes], pow_step_ref[n_sub + i:n_sub + i + 1, lanes], cr, ci)
            x_re_ref[rows, lanes] = x_re_ref[rows, lanes] + pr
            x_im_ref[rows, lanes] = x_im_ref[rows, lanes] + pi
        return carry

    lax.fori_loop(0, n_strips, fix_strip, 0)

    ys = []
    for sl in range(SSM_SLABS):
        cols = slice(sl * SSM_SLAB_STATE, (sl + 1) * SSM_SLAB_STATE)
        ys.append(jnp.dot(x_re_ref[:, cols].astype(BF16), c_re_ref[sl], preferred_element_type=F32)
                  - jnp.dot(x_im_ref[:, cols].astype(BF16), c_im_ref[sl], preferred_element_type=F32))
    y = jnp.concatenate(ys, axis=1) + d_ref[...] * u_p.astype(F32)
    y = _gelu_tanh(y)
    gate = jnp.dot(y.astype(BF16), wglu_ref[...], preferred_element_type=F32) + bglu_ref[...]
    y = (y * _sigmoid(gate)).astype(BF16)
    o_ref[...] = jnp.dot(perm_t_ref[...], y, preferred_element_type=F32).astype(BF16)


def _ssm_tables(lam_re, lam_im, log_dt, b_re, b_im, c_re, c_im):
    dt = jnp.exp(log_dt)[:, None]
    a_re = lam_re * dt
    a_im = lam_im * dt

    def lam_pow(n):
        mag = jnp.exp(a_re * n)
        return (mag * jnp.cos(a_im * n)).reshape(-1), (mag * jnp.sin(a_im * n)).reshape(-1)

    lb_re = jnp.exp(a_re) * jnp.cos(a_im)
    lb_im = jnp.exp(a_re) * jnp.sin(a_im)
    den = lam_re * lam_re + lam_im * lam_im
    k_re = ((lb_re - 1.0) * lam_re + lb_im * lam_im) / den
    k_im = (lb_im * lam_re - (lb_re - 1.0) * lam_im) / den
    bb_re = k_re[..., None] * b_re - k_im[..., None] * b_im
    bb_im = k_re[..., None] * b_im + k_im[..., None] * b_re

    g, p, hg = b_re.shape
    gs = g // SSM_SLABS
    eye = jnp.eye(gs, dtype=F32)

    def b_slabs(x):
        x = x.reshape(SSM_SLABS, gs, p, hg)
        return jnp.einsum('sgph,gk->sghkp', x, eye).reshape(SSM_SLABS, gs * hg, gs * p).astype(BF16)

    def c_slabs(x):
        x = x.reshape(SSM_SLABS, gs, hg, p)
        return jnp.einsum('sghp,gk->sgpkh', x, eye).reshape(SSM_SLABS, gs * p, gs * hg).astype(BF16)

    lam = jnp.stack(lam_pow(1.0))
    lam_sub = jnp.concatenate([jnp.stack(lam_pow(float(SSM_SUB * d))) for d in (1, 2, 4)])
    sub_pows = [lam_pow(float(SSM_SUB * (j + 1))) for j in range(SUBLANES)]
    pow_sub = jnp.stack([x[0] for x in sub_pows] + [x[1] for x in sub_pows])
    step_pows = [lam_pow(float(i + 1)) for i in range(SSM_SUB)]
    pow_step = jnp.stack([x[0] for x in step_pows] + [x[1] for x in step_pows])
    return b_slabs(bb_re), b_slabs(bb_im), c_slabs(c_re), c_slabs(c_im), lam, lam_sub, pow_sub, pow_step


def _ssm_permutation():
    perm = np.zeros((SSM_CHUNK, SSM_CHUNK), np.float32)
    for j in range(SUBLANES):
        for i in range(SSM_SUB):
            perm[i * SUBLANES + j, j * SSM_SUB + i] = 1.0
    return jnp.asarray(perm, BF16), jnp.asarray(perm.T, BF16)


def _ssm(z, tables, d_skip, w_glu, b_glu):
    s = z.shape[0]
    w = BRANCH_WIDTH
    perm, perm_t = _ssm_permutation()
    b_re, b_im, c_re, c_im, lam, lam_sub, pow_sub, pow_step = tables
    full = lambda a: pl.BlockSpec(a.shape, lambda c, _n=a.ndim: (0,) * _n)
    operands = (perm, perm_t, b_re, b_im, c_re, c_im, lam, lam_sub, pow_sub, pow_step,
                d_skip.reshape(1, w), w_glu, b_glu.reshape(1, w))
    return pl.pallas_call(
        _ssm_kernel,
        out_shape=jax.ShapeDtypeStruct((s, w), BF16),
        grid=(s // SSM_CHUNK,),
        in_specs=[pl.BlockSpec((SSM_CHUNK, w), lambda c: (c, Z_SSM // w))] + [full(a) for a in operands],
        out_specs=pl.BlockSpec((SSM_CHUNK, w), lambda c: (c, 0)),
        scratch_shapes=[
            pltpu.VMEM((SSM_CHUNK, SSM_STATES), F32),
            pltpu.VMEM((SSM_CHUNK, SSM_STATES), F32),
            pltpu.VMEM((2 * SUBLANES, SSM_STATES), F32),
            pltpu.VMEM((2, SSM_STATES), F32),
        ],
        compiler_params=_params(("arbitrary",)),
        name="s5_ssm",
    )(z, *operands)


def _conv_kernel(val_ref, gate_ref, hval_ref, hgate_ref, w_ref, b_ref, lng_ref, lnb_ref, o_ref,
                 a_ref, sh_ref, y_ref, *, tt, row_chunk, lane_chunk):
    a_ref[CONV_HALO:CONV_HALO + tt, :] = val_ref[...].astype(F32) * _sigmoid(gate_ref[...].astype(F32))
    halo = hval_ref[...].astype(F32) * _sigmoid(hgate_ref[...].astype(F32))
    a_ref[0:CONV_HALO, :] = jnp.where(pl.program_id(0) > 0, halo, 0.0)

    n_shifted = tt + CONV_HALO - SUBLANES
    for r in range(1, SUBLANES):
        sh_ref[r - 1, 0:n_shifted, :] = a_ref[r:r + n_shifted, :]

    first = CONV_HALO - (CONV_WIDTH - 1)
    for c0 in range(0, BRANCH_WIDTH, lane_chunk):
        lanes = slice(c0, c0 + lane_chunk)
        for r0 in range(0, tt, row_chunk):
            acc = jnp.broadcast_to(b_ref[:, lanes], (row_chunk, lane_chunk))
            for j in range(CONV_WIDTH):
                tiles, r = divmod(first + j, SUBLANES)
                rows = slice(r0 + tiles * SUBLANES, r0 + tiles * SUBLANES + row_chunk)
                tap = a_ref[rows, lanes] if r == 0 else sh_ref[r - 1, rows, lanes]
                acc = acc + w_ref[j:j + 1, lanes] * tap
            y_ref[r0:r0 + row_chunk, lanes] = acc

    y = y_ref[...]
    mu = jnp.mean(y, axis=-1, keepdims=True)
    yc = y - mu
    var = jnp.mean(yc * yc, axis=-1, keepdims=True)
    yn = yc * lax.rsqrt(var + EPS) * lng_ref[...] + lnb_ref[...]
    o_ref[...] = (yn * _sigmoid(yn)).astype(BF16)


def _conv(z, w_dw, b_dw, ln_g, ln_b, *, tt=256):
    s = z.shape[0]
    w = BRANCH_WIDTH
    hb = tt // CONV_HALO
    halo_map = lambda col: (lambda i: (jnp.maximum(i * hb - 1, 0), col))
    return pl.pallas_call(
        functools.partial(_conv_kernel, tt=tt, row_chunk=64, lane_chunk=256),
        out_shape=jax.ShapeDtypeStruct((s, w), BF16),
        grid=(s // tt,),
        in_specs=[
            pl.BlockSpec((tt, w), lambda i: (i, Z_CVAL // w)),
            pl.BlockSpec((tt, w), lambda i: (i, Z_CGATE // w)),
            pl.BlockSpec((CONV_HALO, w), halo_map(Z_CVAL // w)),
            pl.BlockSpec((CONV_HALO, w), halo_map(Z_CGATE // w)),
            pl.BlockSpec((CONV_WIDTH, w), lambda i: (0, 0)),
            pl.BlockSpec((1, w), lambda i: (0, 0)),
            pl.BlockSpec((1, w), lambda i: (0, 0)),
            pl.BlockSpec((1, w), lambda i: (0, 0)),
        ],
        out_specs=pl.BlockSpec((tt, w), lambda i: (i, 0)),
        scratch_shapes=[pltpu.VMEM((CONV_HALO + tt, w), F32),
                        pltpu.VMEM((SUBLANES - 1, CONV_HALO + tt, w), F32),
                        pltpu.VMEM((tt, w), F32)],
        compiler_params=_params(("parallel",)),
        name="conformer_conv",
    )(z, z, z, z, w_dw, b_dw.reshape(1, w), ln_g.reshape(1, w), ln_b.reshape(1, w))


def _pool_kernel(p_ref, hp_ref, w_ref, b_ref, sc_ref, o_ref, e_ref, *, tt):
    i = pl.program_id(0)
    e_ref[POOL_HALO:POOL_HALO + tt, :] = p_ref[...].astype(F32)
    e_ref[0:POOL_HALO, :] = jnp.where(i > 0, hp_ref[...].astype(F32), 0.0)
    pos = (i * tt + 1 + lax.broadcasted_iota(jnp.int32, (tt, 1), 0)).astype(F32)
    for g, win in enumerate(POOL_WINDOWS):
        lanes = slice(g * POOL_GROUP, (g + 1) * POOL_GROUP)
        cur = e_ref[POOL_HALO:POOL_HALO + tt, lanes]
        tot = cur
        for lag in range(1, win):
            tot = tot + e_ref[POOL_HALO - lag:POOL_HALO - lag + tt, lanes]
        zg = tot / jnp.minimum(pos, float(win)) - cur
        out = jnp.dot(zg.astype(BF16), w_ref[g], preferred_element_type=F32) + b_ref[g:g + 1, :]
        o_ref[:, lanes] = (out * sc_ref[:, lanes]).astype(BF16)


def _pool(z, w_pool, b_pool, scale, *, tt=512):
    s = z.shape[0]
    w = BRANCH_WIDTH
    hb = tt // POOL_HALO
    return pl.pallas_call(
        functools.partial(_pool_kernel, tt=tt),
        out_shape=jax.ShapeDtypeStruct((s, w), BF16),
        grid=(s // tt,),
        in_specs=[
            pl.BlockSpec((tt, w), lambda i: (i, Z_POOL // w)),
            pl.BlockSpec((POOL_HALO, w), lambda i: (jnp.maximum(i * hb - 1, 0), Z_POOL // w)),
            pl.BlockSpec(w_pool.shape, lambda i: (0, 0, 0)),
            pl.BlockSpec(b_pool.shape, lambda i: (0, 0)),
            pl.BlockSpec((1, w), lambda i: (0, 0)),
        ],
        out_specs=pl.BlockSpec((tt, w), lambda i: (i, 0)),
        scratch_shapes=[pltpu.VMEM((POOL_HALO + tt, w), F32)],
        compiler_params=_params(("parallel",)),
        name="multiscale_pool",
    )(z, z, w_pool, b_pool, scale.reshape(1, w))


def _merge_kernel(u_ref, ya_ref, yb_ref, yc_ref, yd_ref, gw_ref, gb_ref, up_ref, o_ref):
    u = u_ref[...]
    acc = None
    for n, y_ref in enumerate((ya_ref, yb_ref, yc_ref, yd_ref)):
        gate = _sigmoid(jnp.dot(u, gw_ref[n], preferred_element_type=F32) + gb_ref[n:n + 1, :])
        term = gate * jnp.dot(y_ref[...], up_ref[n], preferred_element_type=F32)
        acc = term if acc is None else acc + term
    o_ref[...] = acc.astype(BF16)


def _merge(u, ys, gate_w, gate_b, up_w, *, tm=1024):
    s, d = u.shape
    nb, w = up_w.shape[0], up_w.shape[1]
    gb = GATE_BLOCK
    y_spec = pl.BlockSpec((tm, w), lambda i, k: (i, 0))
    return pl.pallas_call(
        _merge_kernel,
        out_shape=jax.ShapeDtypeStruct((s, d), BF16),
        grid=(s // tm, d // gb),
        in_specs=[
            pl.BlockSpec((tm, gb), lambda i, k: (i, k)),
            y_spec, y_spec, y_spec, y_spec,
            pl.BlockSpec((nb, None, gb, gb), lambda i, k: (0, k, 0, 0)),
            pl.BlockSpec((nb, gb), lambda i, k: (0, k)),
            pl.BlockSpec((nb, w, gb), lambda i, k: (0, 0, k)),
        ],
        out_specs=pl.BlockSpec((tm, gb), lambda i, k: (i, k)),
        compiler_params=_params(("parallel", "arbitrary")),
        name="gated_merge",
    )(u, *ys, gate_w, gate_b, up_w)


def _ffn_block(h, xn, w_gate, w_up, w_down, post_g, g_next):
    act = _ffn_up(xn, w_gate.astype(BF16), w_up.astype(BF16))
    return _proj_norm_res(act, w_down.astype(BF16), post_g, h, 0.5, g_next)


def _mixer_block(h, u, w_in, forget_bias, ssm_params, ssm_d, ssm_w_glu, ssm_b_glu,
                 conv_w, conv_b, conv_ln_g, conv_ln_b, pool_w, pool_b, pool_scale,
                 branch_w_up, gate_w, gate_b, w_out, post_g, g_next):
    off_f = 3 * BRANCH_WIDTH
    w_q = w_in[:, :BRANCH_WIDTH] * (ATT_HEAD_DIM ** -0.5)
    w_main = jnp.concatenate([w_q, w_in[:, BRANCH_WIDTH:off_f], w_in[:, off_f + ATT_HEADS:]], axis=1).astype(BF16)
    w_f = jnp.zeros((HEAD_ROWS, D_MODEL), F32).at[:ATT_HEADS].set(w_in[:, off_f:off_f + ATT_HEADS].T).astype(BF16)
    b_f = jnp.zeros((HEAD_ROWS, 1), F32).at[:ATT_HEADS, 0].set(forget_bias)

    z = _in_proj(u, w_main)
    c = _forget_cumsum(u, w_f, b_f)
    y_a = _attention(z, c)
    y_b = _ssm(z, _ssm_tables(*ssm_params), ssm_d, ssm_w_glu.astype(BF16), ssm_b_glu)
    y_c = _conv(z, conv_w, conv_b, conv_ln_g, conv_ln_b)
    y_d = _pool(z, pool_w.astype(BF16), pool_b, pool_scale)
    merged = _merge(u, (y_a, y_b, y_c, y_d), gate_w.astype(BF16), gate_b, branch_w_up.astype(BF16))
    return _proj_norm_res(merged, w_out.astype(BF16), post_g, h, 1.0, g_next)


def kernel(x, ffn1_pre_g, ffn1_w_gate, ffn1_w_up, ffn1_w_down, ffn1_post_g, mix_pre_g, w_in, forget_bias, ssm_lam_re, ssm_lam_im, ssm_log_dt, ssm_b_re, ssm_b_im, ssm_c_re, ssm_c_im, ssm_d, ssm_w_glu, ssm_b_glu, conv_w, conv_b, conv_ln_g, conv_ln_b, pool_w, pool_b, pool_scale, branch_w_up, gate_w, gate_b, w_out, mix_post_g, ffn2_pre_g, ffn2_w_gate, ffn2_w_up, ffn2_w_down, ffn2_post_g):
    batch, seq, d = x.shape
    outs = []
    for b in range(batch):
        h = x[b]
        depth = ffn1_pre_g.shape[0]
        xn = _rms_norm(h, ffn1_pre_g[0])
        for l in range(depth):
            h, xn = _ffn_block(h, xn, ffn1_w_gate[l], ffn1_w_up[l], ffn1_w_down[l], ffn1_post_g[l], mix_pre_g[l])
            h, xn = _mixer_block(
                h, xn, w_in[l], forget_bias[l],
                (ssm_lam_re[l], ssm_lam_im[l], ssm_log_dt[l], ssm_b_re[l], ssm_b_im[l], ssm_c_re[l], ssm_c_im[l]),
                ssm_d[l], ssm_w_glu[l], ssm_b_glu[l], conv_w[l], conv_b[l], conv_ln_g[l], conv_ln_b[l],
                pool_w[l], pool_b[l], pool_scale[l], branch_w_up[l], gate_w[l], gate_b[l], w_out[l], mix_post_g[l],
                ffn2_pre_g[l])
            g_next = ffn1_pre_g[l + 1] if l + 1 < depth else None
            h, xn = _ffn_block(h, xn, ffn2_w_gate[l], ffn2_w_up[l], ffn2_w_down[l], ffn2_post_g[l], g_next)
        outs.append(h)
    return jnp.stack(outs)
```

```python
import functools
import math

import numpy as np
import jax
import jax.numpy as jnp
from jax import lax
from jax.experimental import pallas as pl
from jax.experimental.pallas import tpu as pltpu

F32 = jnp.float32
BF16 = jnp.bfloat16

D_MODEL = 4096
BRANCH_WIDTH = 1024
ATT_HEADS = 8
ATT_HEAD_DIM = 128
SSM_GROUP = 16
SSM_GROUPS = 64
SSM_STATE = 64
CONV_WIDTH = 31
POOL_WINDOWS = (2, 4, 8, 16)
POOL_GROUP = 256
GATE_BLOCKS = 16
GATE_BLOCK = 256
D_FF = 6144
EPS = 1e-6

V7X_VMEM_BYTES = 64 * 1024 * 1024
VMEM_LIMIT_BYTES = V7X_VMEM_BYTES - 8 * 1024 * 1024
SUBLANES = 8
LANES = 128

Z_Q, Z_K, Z_V, Z_SSM, Z_CVAL, Z_CGATE, Z_POOL = (n * BRANCH_WIDTH for n in range(7))
Z_WIDTH = 7 * BRANCH_WIDTH
HEAD_ROWS = 16

SSM_CHUNK = 256
SSM_SUB = SSM_CHUNK // SUBLANES
SSM_SLABS = 4
SSM_SLAB_IN = BRANCH_WIDTH // SSM_SLABS
SSM_STATES = SSM_GROUPS * SSM_STATE
SSM_SLAB_STATE = SSM_STATES // SSM_SLABS
SSM_STRIP = 512

CONV_HALO = 32
POOL_HALO = 16

NEG_BIG = -1e30
ATT_STABILISER_MARGIN = 60.0


def _params(semantics):
    return pltpu.CompilerParams(dimension_semantics=semantics, vmem_limit_bytes=VMEM_LIMIT_BYTES)


def _sigmoid(x):
    return 1.0 / (1.0 + jnp.exp(-x))


def _rms_scale(x):
    return lax.rsqrt(jnp.mean(x * x, axis=-1, keepdims=True) + EPS)


def _rms_norm_kernel(x_ref, g_ref, o_ref):
    x = x_ref[...]
    o_ref[...] = (x * _rms_scale(x) * g_ref[...]).astype(BF16)


def _rms_norm(x, g, *, tm=256):
    s, d = x.shape
    return pl.pallas_call(
        _rms_norm_kernel,
        out_shape=jax.ShapeDtypeStruct((s, d), BF16),
        grid=(s // tm,),
        in_specs=[pl.BlockSpec((tm, d), lambda i: (i, 0)), pl.BlockSpec((1, d), lambda i: (0, 0))],
        out_specs=pl.BlockSpec((tm, d), lambda i: (i, 0)),
        compiler_params=_params(("parallel",)),
        name="rms_norm",
    )(x, g.reshape(1, d))


def _ffn_up_kernel(x_ref, wg_ref, wu_ref, o_ref):
    x = x_ref[...]
    a = jnp.dot(x, wg_ref[...], preferred_element_type=F32)
    b = jnp.dot(x, wu_ref[...], preferred_element_type=F32)
    o_ref[...] = (a * _sigmoid(a) * b).astype(BF16)


def _ffn_up(xn, wg, wu, *, tm=1024, tn=512):
    s, d = xn.shape
    f = wg.shape[1]
    return pl.pallas_call(
        _ffn_up_kernel,
        out_shape=jax.ShapeDtypeStruct((s, f), BF16),
        grid=(s // tm, f // tn),
        in_specs=[
            pl.BlockSpec((tm, d), lambda i, j: (i, 0)),
            pl.BlockSpec((d, tn), lambda i, j: (0, j)),
            pl.BlockSpec((d, tn), lambda i, j: (0, j)),
        ],
        out_specs=pl.BlockSpec((tm, tn), lambda i, j: (i, j)),
        compiler_params=_params(("parallel", "arbitrary")),
        name="ffn_up",
    )(xn, wg, wu)


PROJ_DMA_STEP = 2


def _proj_norm_res_kernel(*refs, alpha, tm, col_chunk, row_chunk, emit_xn):
    if emit_xn:
        a_ref, w_ref, g_ref, gn_ref, h_hbm, ho_hbm, xn_hbm, acc_ref, hbuf_ref, xnbuf_ref, sems = refs
    else:
        a_ref, w_ref, g_ref, h_hbm, ho_hbm, acc_ref, hbuf_ref, sems = refs
    i = pl.program_id(0)
    k = pl.program_id(1)
    n_i = pl.num_programs(0)
    n_k = pl.num_programs(1)
    d = acc_ref.shape[1]

    def block_rows(ref, blk):
        return ref.at[pl.ds(pl.multiple_of(blk * tm, tm), tm), :]

    def h_in_copy(blk):
        return pltpu.make_async_copy(block_rows(h_hbm, blk), hbuf_ref, sems.at[0])

    def result_copies(blk):
        copies = [pltpu.make_async_copy(hbuf_ref, block_rows(ho_hbm, blk), sems.at[1])]
        if emit_xn:
            copies.append(pltpu.make_async_copy(xnbuf_ref, block_rows(xn_hbm, blk), sems.at[2]))
        return copies

    def partial_products(accumulate):
        a = a_ref[...]
        for c0 in range(0, d, col_chunk):
            cols = slice(c0, c0 + col_chunk)
            t = jnp.dot(a, w_ref[:, cols], preferred_element_type=F32)
            if accumulate:
                acc_ref[:, cols] += t
            else:
                acc_ref[:, cols] = t

    @pl.when(k == 0)
    def _():
        partial_products(False)

    @pl.when(k > 0)
    def _():
        partial_products(True)

    @pl.when(k == PROJ_DMA_STEP)
    def _():
        @pl.when(i > 0)
        def _():
            for cp in result_copies(i - 1):
                cp.wait()

        h_in_copy(i).start()

    @pl.when(k == n_k - 1)
    def _():
        h_in_copy(i).wait()
        ga = alpha * g_ref[...]
        for r0 in range(0, tm, row_chunk):
            rows = slice(r0, r0 + row_chunk)
            acc = acc_ref[rows, :]
            h_new = hbuf_ref[rows, :] + acc * _rms_scale(acc) * ga
            hbuf_ref[rows, :] = h_new
            if emit_xn:
                xnbuf_ref[rows, :] = (h_new * _rms_scale(h_new) * gn_ref[...]).astype(BF16)
        for cp in result_copies(i):
            cp.start()

        @pl.when(i == n_i - 1)
        def _():
            for cp in result_copies(i):
                cp.wait()


def _proj_norm_res(a, w, g, h, alpha, g_next, *, tm=1024, tk=512):
    s, kdim = a.shape
    d = w.shape[1]
    emit_xn = g_next is not None
    assert kdim // tk > PROJ_DMA_STEP, "the copy schedule needs more than PROJ_DMA_STEP contraction steps"
    gain_spec = pl.BlockSpec((1, d), lambda i, k: (0, 0))
    any_spec = pl.BlockSpec(memory_space=pl.ANY)
    gains = [g.reshape(1, d)] + ([g_next.reshape(1, d)] if emit_xn else [])
    out_shape = [jax.ShapeDtypeStruct((s, d), F32)] + ([jax.ShapeDtypeStruct((s, d), BF16)] if emit_xn else [])
    scratch = [pltpu.VMEM((tm, d), F32), pltpu.VMEM((tm, d), F32)]
    scratch += [pltpu.VMEM((tm, d), BF16)] if emit_xn else []
    scratch += [pltpu.SemaphoreType.DMA((3,))]
    outs = pl.pallas_call(
        functools.partial(_proj_norm_res_kernel, alpha=alpha, tm=tm, col_chunk=512, row_chunk=64, emit_xn=emit_xn),
        out_shape=out_shape,
        grid=(s // tm, kdim // tk),
        in_specs=[pl.BlockSpec((tm, tk), lambda i, k: (i, k)), pl.BlockSpec((tk, d), lambda i, k: (k, 0))]
        + [gain_spec] * len(gains) + [any_spec],
        out_specs=[any_spec] * len(out_shape),
        scratch_shapes=scratch,
        compiler_params=_params(("arbitrary", "arbitrary")),
        name="proj_norm_res" if emit_xn else "proj_norm_res_last",
    )(a, w, *gains, h)
    return (outs[0], outs[1]) if emit_xn else (outs[0], None)


def _in_proj_kernel(u_ref, w_ref, z_ref):
    z_ref[...] = jnp.dot(u_ref[...], w_ref[...], preferred_element_type=F32).astype(BF16)


def _in_proj(u, w, *, tm=1024, tn=1024):
    s, d = u.shape
    n = w.shape[1]
    return pl.pallas_call(
        _in_proj_kernel,
        out_shape=jax.ShapeDtypeStruct((s, n), BF16),
        grid=(s // tm, n // tn),
        in_specs=[
            pl.BlockSpec((tm, d), lambda i, j: (i, 0)),
            pl.BlockSpec((d, tn), lambda i, j: (0, j)),
        ],
        out_specs=pl.BlockSpec((tm, tn), lambda i, j: (i, j)),
        compiler_params=_params(("parallel", "arbitrary")),
        name="in_proj",
    )(u, w)


def _split_bf16(x, parts=3):
    out = []
    for _ in range(parts):
        p = x.astype(BF16)
        out.append(p)
        x = x - p.astype(F32)
    return out


def _forget_kernel(u_ref, wf_ref, bf_ref, tri_ref, c_ref, carry_ref):
    @pl.when(pl.program_id(0) == 0)
    def _():
        carry_ref[...] = jnp.zeros_like(carry_ref)

    f = lax.dot_general(wf_ref[...], u_ref[...], (((1,), (1,)), ((), ())),
                        preferred_element_type=F32)
    x = f + bf_ref[...]
    log_f = jnp.minimum(x, 0.0) - jnp.log(1.0 + jnp.exp(-jnp.abs(x)))
    tri = tri_ref[...]
    cs = sum(jnp.dot(p, tri, preferred_element_type=F32) for p in _split_bf16(log_f))
    c = cs + carry_ref[...]
    c_ref[...] = c
    carry_ref[...] = c[:, c.shape[1] - 1:]


def _forget_cumsum(u, wf_t, bf, *, tm=512):
    s, d = u.shape
    tri = jnp.asarray(np.triu(np.ones((tm, tm), np.float32)), BF16)
    return pl.pallas_call(
        _forget_kernel,
        out_shape=jax.ShapeDtypeStruct((HEAD_ROWS, s), F32),
        grid=(s // tm,),
        in_specs=[
            pl.BlockSpec((tm, d), lambda i: (i, 0)),
            pl.BlockSpec((HEAD_ROWS, d), lambda i: (0, 0)),
            pl.BlockSpec((HEAD_ROWS, 1), lambda i: (0, 0)),
            pl.BlockSpec((tm, tm), lambda i: (0, 0)),
        ],
        out_specs=pl.BlockSpec((HEAD_ROWS, tm), lambda i: (0, i)),
        scratch_shapes=[pltpu.VMEM((HEAD_ROWS, 1), F32)],
        compiler_params=_params(("arbitrary",)),
        name="forget_cumsum",
    )(u, wf_t, bf, tri)


def _lane_groups(x):
    return [x[:, g * LANES:(g + 1) * LANES] for g in range(x.shape[1] // LANES)]


def _max_row_sq_norm(x):
    xf = x.astype(F32)
    return jnp.max(jnp.sum(xf * xf, axis=1, keepdims=True), axis=0, keepdims=True)


def _attn_kernel(q_ref, k_ref, v_ref, c_ref, o_ref, vaug_ref, kn_ref, m_ref, acc_ref, sd_ref, *, tq):
    h = pl.program_id(0)
    i = pl.program_id(1)
    dh = ATT_HEAD_DIM
    seq = k_ref.shape[0]

    @pl.when(i == 0)
    def _():
        vaug_ref[:, 0:dh] = v_ref[...]
        vaug_ref[:, dh:2 * dh] = jnp.ones((seq, dh), BF16)

        def kn_body(t, mx):
            return jnp.maximum(mx, _max_row_sq_norm(k_ref[pl.ds(pl.multiple_of(t * tq, tq), tq), :]))

        kn2 = lax.fori_loop(0, seq // tq, kn_body, jnp.zeros((1, 1), F32))
        kn_ref[...] = jnp.broadcast_to(kn2, kn_ref.shape)

    q = q_ref[...]
    start = pl.multiple_of(i * tq, tq)
    c_q = c_ref[pl.ds(h, 1), pl.ds(start, tq)]
    c_base = c_q[:, 0:1]
    nt_dims = (((1,), (1,)), ((), ()))

    def scores(off, width):
        c_k = c_ref[pl.ds(h, 1), pl.ds(off, width)] - c_base
        return lax.dot_general(q, k_ref[pl.ds(off, width), :], nt_dims, preferred_element_type=F32) - c_k

    def diag_scores():
        rows = lax.broadcasted_iota(jnp.int32, (tq, tq), 0)
        cols = lax.broadcasted_iota(jnp.int32, (tq, tq), 1)
        return jnp.where(cols <= rows, scores(start, tq), NEG_BIG)

    def lane_max(s):
        return functools.reduce(jnp.maximum, _lane_groups(s))

    def probs(s):
        m_rep = m_ref[...]
        return jnp.concatenate([jnp.exp(g - m_rep) for g in _lane_groups(s)], axis=1).astype(BF16)

    def update_max(off, width):
        m_ref[...] = jnp.maximum(m_ref[...], lane_max(scores(off, width)))

    def accumulate(off, width):
        acc_ref[...] += jnp.dot(probs(scores(off, width)), vaug_ref[pl.ds(off, width), :],
                                preferred_element_type=F32)

    bound = jnp.sqrt(_max_row_sq_norm(q) * kn_ref[0:1, 0:1]) * 1.001
    threshold = 2.0 * bound - ATT_STABILISER_MARGIN
    pos = lax.broadcasted_iota(jnp.int32, (1, seq), 1)
    block_end = (pos & (tq - 1)) == (tq - 1)
    near = jnp.where(block_end, c_ref[pl.ds(h, 1), :] - c_base, threshold) < threshold
    n_near = jnp.sum(jnp.where(near & (pos < start), 1.0, 0.0)).astype(jnp.int32)

    sd_ref[...] = diag_scores()
    m_ref[...] = lane_max(sd_ref[...])

    def near_body(t, carry):
        update_max(pl.multiple_of((i - 1 - t) * tq, tq), tq)
        return carry

    lax.fori_loop(0, n_near, near_body, 0)
    m_ref[...] = jnp.broadcast_to(jnp.max(m_ref[...], axis=1, keepdims=True), m_ref.shape)

    acc_ref[...] = jnp.dot(probs(sd_ref[...]), vaug_ref[pl.ds(start, tq), :], preferred_element_type=F32)

    def wide_body(jj, carry):
        accumulate(pl.multiple_of(jj * (2 * tq), 2 * tq), 2 * tq)
        return carry

    lax.fori_loop(0, i // 2, wide_body, 0)

    @pl.when((i & 1) != 0)
    def _():
        accumulate(pl.multiple_of((i - 1) * tq, tq), tq)

    o_ref[...] = (acc_ref[:, 0:dh] / acc_ref[:, dh:dh + 1]).astype(BF16)


def _attention(z, c, *, tq=1024):
    s = z.shape[0]
    dh = ATT_HEAD_DIM
    return pl.pallas_call(
        functools.partial(_attn_kernel, tq=tq),
        out_shape=jax.ShapeDtypeStruct((s, BRANCH_WIDTH), BF16),
        grid=(ATT_HEADS, s // tq),
        in_specs=[
            pl.BlockSpec((tq, dh), lambda h, i: (i, Z_Q // dh + h)),
            pl.BlockSpec((s, dh), lambda h, i: (0, Z_K // dh + h)),
            pl.BlockSpec((s, dh), lambda h, i: (0, Z_V // dh + h)),
            pl.BlockSpec((HEAD_ROWS, s), lambda h, i: (0, 0)),
        ],
        out_specs=pl.BlockSpec((tq, dh), lambda h, i: (i, h)),
        scratch_shapes=[pltpu.VMEM((s, 2 * dh), BF16), pltpu.VMEM((SUBLANES, LANES), F32),
                        pltpu.VMEM((tq, LANES), F32),
                        pltpu.VMEM((tq, 2 * dh), F32),
                        pltpu.VMEM((tq, tq), F32)],
        compiler_params=_params(("parallel", "arbitrary")),
        name="forget_attention",
    )(z, z, z, c)


def _cmul(ar, ai, br, bi):
    return ar * br - ai * bi, ar * bi + ai * br


def _gelu_tanh(x):
    return 0.5 * x * (1.0 + jnp.tanh(math.sqrt(2.0 / math.pi) * (x + 0.044715 * (x * x * x))))


def _ssm_kernel(u_ref, perm_ref, perm_t_ref, b_re_ref, b_im_ref, c_re_ref, c_im_ref,
                lam_ref, lam_sub_ref, pow_sub_ref, pow_step_ref, d_ref, wglu_ref, bglu_ref,
                o_ref, x_re_ref, x_im_ref, cin_ref, carry_ref):
    @pl.when(pl.program_id(0) == 0)
    def _():
        carry_ref[...] = jnp.zeros_like(carry_ref)

    n_sub = SSM_SUB
    n_strips = SSM_STATES // SSM_STRIP
    u_p = jnp.dot(perm_ref[...], u_ref[...], preferred_element_type=F32).astype(BF16)

    for sl in range(SSM_SLABS):
        u_sl = u_p[:, sl * SSM_SLAB_IN:(sl + 1) * SSM_SLAB_IN]
        cols = slice(sl * SSM_SLAB_STATE, (sl + 1) * SSM_SLAB_STATE)
        x_re_ref[:, cols] = jnp.dot(u_sl, b_re_ref[sl], preferred_element_type=F32)
        x_im_ref[:, cols] = jnp.dot(u_sl, b_im_ref[sl], preferred_element_type=F32)

    def scan_strip(st, carry):
        lanes = pl.ds(pl.multiple_of(st * SSM_STRIP, SSM_STRIP), SSM_STRIP)
        lr = jnp.broadcast_to(lam_ref[0:1, lanes], (SUBLANES, SSM_STRIP))
        li = jnp.broadcast_to(lam_ref[1:2, lanes], (SUBLANES, SSM_STRIP))
        sr = x_re_ref[0:SUBLANES, lanes]
        si = x_im_ref[0:SUBLANES, lanes]
        for i in range(1, n_sub):
            rows = slice(i * SUBLANES, (i + 1) * SUBLANES)
            pr, pi = _cmul(lr, li, sr, si)
            sr = pr + x_re_ref[rows, lanes]
            si = pi + x_im_ref[rows, lanes]
            x_re_ref[rows, lanes] = sr
            x_im_ref[rows, lanes] = si
        return carry

    lax.fori_loop(0, n_strips, scan_strip, 0)

    last = slice((n_sub - 1) * SUBLANES, n_sub * SUBLANES)
    tr = x_re_ref[last, :]
    ti = x_im_ref[last, :]
    row = lax.broadcasted_iota(jnp.int32, (SUBLANES, SSM_STATES), 0)
    for lvl, dist in enumerate((1, 2, 4)):
        ar = lam_sub_ref[2 * lvl:2 * lvl + 1, :]
        ai = lam_sub_ref[2 * lvl + 1:2 * lvl + 2, :]
        shr = jnp.where(row >= dist, pltpu.roll(tr, dist, 0), 0.0)
        shi = jnp.where(row >= dist, pltpu.roll(ti, dist, 0), 0.0)
        pr, pi = _cmul(ar, ai, shr, shi)
        tr = tr + pr
        ti = ti + pi
    in_r = carry_ref[0:1, :]
    in_i = carry_ref[1:2, :]
    pr, pi = _cmul(pow_sub_ref[0:SUBLANES, :], pow_sub_ref[SUBLANES:2 * SUBLANES, :], in_r, in_i)
    tr = tr + pr
    ti = ti + pi
    cin_ref[0:SUBLANES, :] = jnp.where(row >= 1, pltpu.roll(tr, 1, 0), in_r)
    cin_ref[SUBLANES:2 * SUBLANES, :] = jnp.where(row >= 1, pltpu.roll(ti, 1, 0), in_i)
    carry_ref[0:1, :] = tr[SUBLANES - 1:SUBLANES, :]
    carry_ref[1:2, :] = ti[SUBLANES - 1:SUBLANES, :]

    def fix_strip(st, carry):
        lanes = pl.ds(pl.multiple_of(st * SSM_STRIP, SSM_STRIP), SSM_STRIP)
        cr = cin_ref[0:SUBLANES, lanes]
        ci = cin_ref[SUBLANES:2 * SUBLANES, lanes]
        for i in range(n_sub):
            rows = slice(i * SUBLANES, (i + 1) * SUBLANES)
            pr, pi = _cmul(pow_step_ref[i:i + 1, lanes], pow_step_ref[n_sub + i:n_sub + i + 1, lanes], cr, ci)
            x_re_ref[rows, lanes] = x_re_ref[rows, lanes] + pr
            x_im_ref[rows, lanes] = x_im_ref[rows, lanes] + pi
        return carry

    lax.fori_loop(0, n_strips, fix_strip, 0)

    ys = []
    for sl in range(SSM_SLABS):
        cols = slice(sl * SSM_SLAB_STATE, (sl + 1) * SSM_SLAB_STATE)
        ys.append(jnp.dot(x_re_ref[:, cols].astype(BF16), c_re_ref[sl], preferred_element_type=F32)
                  - jnp.dot(x_im_ref[:, cols].astype(BF16), c_im_ref[sl], preferred_element_type=F32))
    y = jnp.concatenate(ys, axis=1) + d_ref[...] * u_p.astype(F32)
    y = _gelu_tanh(y)
    gate = jnp.dot(y.astype(BF16), wglu_ref[...], preferred_element_type=F32) + bglu_ref[...]
    y = (y * _sigmoid(gate)).astype(BF16)
    o_ref[...] = jnp.dot(perm_t_ref[...], y, preferred_element_type=F32).astype(BF16)


def _ssm_tables(lam_re, lam_im, log_dt, b_re, b_im, c_re, c_im):
    dt = jnp.exp(log_dt)[:, None]
    a_re = lam_re * dt
    a_im = lam_im * dt

    def lam_pow(n):
        mag = jnp.exp(a_re * n)
        return (mag * jnp.cos(a_im * n)).reshape(-1), (mag * jnp.sin(a_im * n)).reshape(-1)

    lb_re = jnp.exp(a_re) * jnp.cos(a_im)
    lb_im = jnp.exp(a_re) * jnp.sin(a_im)
    den = lam_re * lam_re + lam_im * lam_im
    k_re = ((lb_re - 1.0) * lam_re + lb_im * lam_im) / den
    k_im = (lb_im * lam_re - (lb_re - 1.0) * lam_im) / den
    bb_re = k_re[..., None] * b_re - k_im[..., None] * b_im
    bb_im = k_re[..., None] * b_im + k_im[..., None] * b_re

    g, p, hg = b_re.shape
    gs = g // SSM_SLABS
    eye = jnp.eye(gs, dtype=F32)

    def b_slabs(x):
        x = x.reshape(SSM_SLABS, gs, p, hg)
        return jnp.einsum('sgph,gk->sghkp', x, eye).reshape(SSM_SLABS, gs * hg, gs * p).astype(BF16)

    def c_slabs(x):
        x = x.reshape(SSM_SLABS, gs, hg, p)
        return jnp.einsum('sghp,gk->sgpkh', x, eye).reshape(SSM_SLABS, gs * p, gs * hg).astype(BF16)

    lam = jnp.stack(lam_pow(1.0))
    lam_sub = jnp.concatenate([jnp.stack(lam_pow(float(SSM_SUB * d))) for d in (1, 2, 4)])
    sub_pows = [lam_pow(float(SSM_SUB * (j + 1))) for j in range(SUBLANES)]
    pow_sub = jnp.stack([x[0] for x in sub_pows] + [x[1] for x in sub_pows])
    step_pows = [lam_pow(float(i + 1)) for i in range(SSM_SUB)]
    pow_step = jnp.stack([x[0] for x in step_pows] + [x[1] for x in step_pows])
    return b_slabs(bb_re), b_slabs(bb_im), c_slabs(c_re), c_slabs(c_im), lam, lam_sub, pow_sub, pow_step


def _ssm_permutation():
    perm = np.zeros((SSM_CHUNK, SSM_CHUNK), np.float32)
    for j in range(SUBLANES):
        for i in range(SSM_SUB):
            perm[i * SUBLANES + j, j * SSM_SUB + i] = 1.0
    return jnp.asarray(perm, BF16), jnp.asarray(perm.T, BF16)


def _ssm(z, tables, d_skip, w_glu, b_glu):
    s = z.shape[0]
    w = BRANCH_WIDTH
    perm, perm_t = _ssm_permutation()
    b_re, b_im, c_re, c_im, lam, lam_sub, pow_sub, pow_step = tables
    full = lambda a: pl.BlockSpec(a.shape, lambda c, _n=a.ndim: (0,) * _n)
    operands = (perm, perm_t, b_re, b_im, c_re, c_im, lam, lam_sub, pow_sub, pow_step,
                d_skip.reshape(1, w), w_glu, b_glu.reshape(1, w))
    return pl.pallas_call(
        _ssm_kernel,
        out_shape=jax.ShapeDtypeStruct((s, w), BF16),
        grid=(s // SSM_CHUNK,),
        in_specs=[pl.BlockSpec((SSM_CHUNK, w), lambda c: (c, Z_SSM // w))] + [full(a) for a in operands],
        out_specs=pl.BlockSpec((SSM_CHUNK, w), lambda c: (c, 0)),
        scratch_shapes=[
            pltpu.VMEM((SSM_CHUNK, SSM_STATES), F32),
            pltpu.VMEM((SSM_CHUNK, SSM_STATES), F32),
            pltpu.VMEM((2 * SUBLANES, SSM_STATES), F32),
            pltpu.VMEM((2, SSM_STATES), F32),
        ],
        compiler_params=_params(("arbitrary",)),
        name="s5_ssm",
    )(z, *operands)


def _conv_kernel(val_ref, gate_ref, hval_ref, hgate_ref, w_ref, b_ref, lng_ref, lnb_ref, o_ref,
                 a_ref, sh_ref, y_ref, *, tt, row_chunk, lane_chunk):
    a_ref[CONV_HALO:CONV_HALO + tt, :] = val_ref[...].astype(F32) * _sigmoid(gate_ref[...].astype(F32))
    halo = hval_ref[...].astype(F32) * _sigmoid(hgate_ref[...].astype(F32))
    a_ref[0:CONV_HALO, :] = jnp.where(pl.program_id(0) > 0, halo, 0.0)

    n_shifted = tt + CONV_HALO - SUBLANES
    for r in range(1, SUBLANES):
        sh_ref[r - 1, 0:n_shifted, :] = a_ref[r:r + n_shifted, :]

    first = CONV_HALO - (CONV_WIDTH - 1)
    for c0 in range(0, BRANCH_WIDTH, lane_chunk):
        lanes = slice(c0, c0 + lane_chunk)
        for r0 in range(0, tt, row_chunk):
            acc = jnp.broadcast_to(b_ref[:, lanes], (row_chunk, lane_chunk))
            for j in range(CONV_WIDTH):
                tiles, r = divmod(first + j, SUBLANES)
                rows = slice(r0 + tiles * SUBLANES, r0 + tiles * SUBLANES + row_chunk)
                tap = a_ref[rows, lanes] if r == 0 else sh_ref[r - 1, rows, lanes]
                acc = acc + w_ref[j:j + 1, lanes] * tap
            y_ref[r0:r0 + row_chunk, lanes] = acc

    y = y_ref[...]
    mu = jnp.mean(y, axis=-1, keepdims=True)
    yc = y - mu
    var = jnp.mean(yc * yc, axis=-1, keepdims=True)
    yn = yc * lax.rsqrt(var + EPS) * lng_ref[...] + lnb_ref[...]
    o_ref[...] = (yn * _sigmoid(yn)).astype(BF16)


def _conv(z, w_dw, b_dw, ln_g, ln_b, *, tt=256):
    s = z.shape[0]
    w = BRANCH_WIDTH
    hb = tt // CONV_HALO
    halo_map = lambda col: (lambda i: (jnp.maximum(i * hb - 1, 0), col))
    return pl.pallas_call(
        functools.partial(_conv_kernel, tt=tt, row_chunk=64, lane_chunk=256),
        out_shape=jax.ShapeDtypeStruct((s, w), BF16),
        grid=(s // tt,),
        in_specs=[
            pl.BlockSpec((tt, w), lambda i: (i, Z_CVAL // w)),
            pl.BlockSpec((tt, w), lambda i: (i, Z_CGATE // w)),
            pl.BlockSpec((CONV_HALO, w), halo_map(Z_CVAL // w)),
            pl.BlockSpec((CONV_HALO, w), halo_map(Z_CGATE // w)),
            pl.BlockSpec((CONV_WIDTH, w), lambda i: (0, 0)),
            pl.BlockSpec((1, w), lambda i: (0, 0)),
            pl.BlockSpec((1, w), lambda i: (0, 0)),
            pl.BlockSpec((1, w), lambda i: (0, 0)),
        ],
        out_specs=pl.BlockSpec((tt, w), lambda i: (i, 0)),
        scratch_shapes=[pltpu.VMEM((CONV_HALO + tt, w), F32),
                        pltpu.VMEM((SUBLANES - 1, CONV_HALO + tt, w), F32),
                        pltpu.VMEM((tt, w), F32)],
        compiler_params=_params(("parallel",)),
        name="conformer_conv",
    )(z, z, z, z, w_dw, b_dw.reshape(1, w), ln_g.reshape(1, w), ln_b.reshape(1, w))


def _pool_kernel(p_ref, hp_ref, w_ref, b_ref, sc_ref, o_ref, e_ref, *, tt):
    i = pl.program_id(0)
    e_ref[POOL_HALO:POOL_HALO + tt, :] = p_ref[...].astype(F32)
    e_ref[0:POOL_HALO, :] = jnp.where(i > 0, hp_ref[...].astype(F32), 0.0)
    pos = (i * tt + 1 + lax.broadcasted_iota(jnp.int32, (tt, 1), 0)).astype(F32)
    for g, win in enumerate(POOL_WINDOWS):
        lanes = slice(g * POOL_GROUP, (g + 1) * POOL_GROUP)
        cur = e_ref[POOL_HALO:POOL_HALO + tt, lanes]
        tot = cur
        for lag in range(1, win):
            tot = tot + e_ref[POOL_HALO - lag:POOL_HALO - lag + tt, lanes]
        zg = tot / jnp.minimum(pos, float(win)) - cur
        out = jnp.dot(zg.astype(BF16), w_ref[g], preferred_element_type=F32) + b_ref[g:g + 1, :]
        o_ref[:, lanes] = (out * sc_ref[:, lanes]).astype(BF16)


def _pool(z, w_pool, b_pool, scale, *, tt=512):
    s = z.shape[0]
    w = BRANCH_WIDTH
    hb = tt // POOL_HALO
    return pl.pallas_call(
        functools.partial(_pool_kernel, tt=tt),
        out_shape=jax.ShapeDtypeStruct((s, w), BF16),
        grid=(s // tt,),
        in_specs=[
            pl.BlockSpec((tt, w), lambda i: (i, Z_POOL // w)),
            pl.BlockSpec((POOL_HALO, w), lambda i: (jnp.maximum(i * hb - 1, 0), Z_POOL // w)),
            pl.BlockSpec(w_pool.shape, lambda i: (0, 0, 0)),
            pl.BlockSpec(b_pool.shape, lambda i: (0, 0)),
            pl.BlockSpec((1, w), lambda i: (0, 0)),
        ],
        out_specs=pl.BlockSpec((tt, w), lambda i: (i, 0)),
        scratch_shapes=[pltpu.VMEM((POOL_HALO + tt, w), F32)],
        compiler_params=_params(("parallel",)),
        name="multiscale_pool",
    )(z, z, w_pool, b_pool, scale.reshape(1, w))


def _merge_kernel(u_ref, ya_ref, yb_ref, yc_ref, yd_ref, gw_ref, gb_ref, up_ref, o_ref):
    u = u_ref[...]
    acc = None
    for n, y_ref in enumerate((ya_ref, yb_ref, yc_ref, yd_ref)):
        gate = _sigmoid(jnp.dot(u, gw_ref[n], preferred_element_type=F32) + gb_ref[n:n + 1, :])
        term = gate * jnp.dot(y_ref[...], up_ref[n], preferred_element_type=F32)
        acc = term if acc is None else acc + term
    o_ref[...] = acc.astype(BF16)


def _merge(u, ys, gate_w, gate_b, up_w, *, tm=1024):
    s, d = u.shape
    nb, w = up_w.shape[0], up_w.shape[1]
    gb = GATE_BLOCK
    y_spec = pl.BlockSpec((tm, w), lambda i, k: (i, 0))
    return pl.pallas_call(
        _merge_kernel,
        out_shape=jax.ShapeDtypeStruct((s, d), BF16),
        grid=(s // tm, d // gb),
        in_specs=[
            pl.BlockSpec((tm, gb), lambda i, k: (i, k)),
            y_spec, y_spec, y_spec, y_spec,
            pl.BlockSpec((nb, None, gb, gb), lambda i, k: (0, k, 0, 0)),
            pl.BlockSpec((nb, gb), lambda i, k: (0, k)),
            pl.BlockSpec((nb, w, gb), lambda i, k: (0, 0, k)),
        ],
        out_specs=pl.BlockSpec((tm, gb), lambda i, k: (i, k)),
        compiler_params=_params(("parallel", "arbitrary")),
        name="gated_merge",
    )(u, *ys, gate_w, gate_b, up_w)


def _cast_kernel(x_ref, o_ref):
    o_ref[...] = x_ref[...].astype(BF16)


def _layer_bf16(w, layer=0, *, tr=256):
    if w.ndim == 2:
        w = w[None]
    _, r, c = w.shape
    return pl.pallas_call(
        _cast_kernel,
        out_shape=jax.ShapeDtypeStruct((r, c), BF16),
        grid=(r // tr,),
        in_specs=[pl.BlockSpec((None, tr, c), lambda i: (layer, i, 0))],
        out_specs=pl.BlockSpec((tr, c), lambda i: (i, 0)),
        compiler_params=_params(("parallel",)),
        name="cast_bf16",
    )(w)


def _ffn_block(h, xn, w_gate, w_up, w_down, post_g, g_next, layer=0):
    act = _ffn_up(xn, _layer_bf16(w_gate, layer), _layer_bf16(w_up, layer))
    return _proj_norm_res(act, _layer_bf16(w_down, layer), post_g, h, 0.5, g_next)


def _mixer_block(h, u, w_in, forget_bias, ssm_params, ssm_d, ssm_w_glu, ssm_b_glu,
                 conv_w, conv_b, conv_ln_g, conv_ln_b, pool_w, pool_b, pool_scale,
                 branch_w_up, gate_w, gate_b, w_out, post_g, g_next, layer=0):
    if branch_w_up.ndim == 3:
        branch_w_up = branch_w_up[None]
    n_br, br_w, d_out = branch_w_up.shape[1:]
    up_w = _layer_bf16(branch_w_up.reshape(-1, n_br * br_w, d_out), layer).reshape(n_br, br_w, d_out)
    off_f = 3 * BRANCH_WIDTH
    w_q = w_in[:, :BRANCH_WIDTH] * (ATT_HEAD_DIM ** -0.5)
    w_main = jnp.concatenate([w_q, w_in[:, BRANCH_WIDTH:off_f], w_in[:, off_f + ATT_HEADS:]], axis=1).astype(BF16)
    w_f = jnp.zeros((HEAD_ROWS, D_MODEL), F32).at[:ATT_HEADS].set(w_in[:, off_f:off_f + ATT_HEADS].T).astype(BF16)
    b_f = jnp.zeros((HEAD_ROWS, 1), F32).at[:ATT_HEADS, 0].set(forget_bias)

    z = _in_proj(u, w_main)
    c = _forget_cumsum(u, w_f, b_f)
    y_a = _attention(z, c)
    y_b = _ssm(z, _ssm_tables(*ssm_params), ssm_d, ssm_w_glu.astype(BF16), ssm_b_glu)
    y_c = _conv(z, conv_w, conv_b, conv_ln_g, conv_ln_b)
    y_d = _pool(z, pool_w.astype(BF16), pool_b, pool_scale)
    merged = _merge(u, (y_a, y_b, y_c, y_d), gate_w.astype(BF16), gate_b, up_w)
    return _proj_norm_res(merged, _layer_bf16(w_out, layer), post_g, h, 1.0, g_next)


def kernel(x, ffn1_pre_g, ffn1_w_gate, ffn1_w_up, ffn1_w_down, ffn1_post_g, mix_pre_g, w_in, forget_bias, ssm_lam_re, ssm_lam_im, ssm_log_dt, ssm_b_re, ssm_b_im, ssm_c_re, ssm_c_im, ssm_d, ssm_w_glu, ssm_b_glu, conv_w, conv_b, conv_ln_g, conv_ln_b, pool_w, pool_b, pool_scale, branch_w_up, gate_w, gate_b, w_out, mix_post_g, ffn2_pre_g, ffn2_w_gate, ffn2_w_up, ffn2_w_down, ffn2_post_g):
    batch, seq, d = x.shape
    outs = []
    for b in range(batch):
        h = x[b]
        depth = ffn1_pre_g.shape[0]
        xn = _rms_norm(h, ffn1_pre_g[0])
        for l in range(depth):
            h, xn = _ffn_block(h, xn, ffn1_w_gate, ffn1_w_up, ffn1_w_down, ffn1_post_g[l], mix_pre_g[l], layer=l)
            h, xn = _mixer_block(
                h, xn, w_in[l], forget_bias[l],
                (ssm_lam_re[l], ssm_lam_im[l], ssm_log_dt[l], ssm_b_re[l], ssm_b_im[l], ssm_c_re[l], ssm_c_im[l]),
                ssm_d[l], ssm_w_glu[l], ssm_b_glu[l], conv_w[l], conv_b[l], conv_ln_g[l], conv_ln_b[l],
                pool_w[l], pool_b[l], pool_scale[l], branch_w_up, gate_w[l], gate_b[l], w_out, mix_post_g[l],
                ffn2_pre_g[l], layer=l)
            g_next = ffn1_pre_g[l + 1] if l + 1 < depth else None
            h, xn = _ffn_block(h, xn, ffn2_w_gate, ffn2_w_up, ffn2_w_down, ffn2_post_g[l], g_next, layer=l)
        outs.append(h)
    return jnp.stack(outs)
```
